```python
import jax, jax.numpy as jnp
from jax import lax
import numpy as np

D_MODEL = 2048
BATCH = 4
SEQ = 2048
DEPTH = 2
DEC_BATCH = 128
DEC_SEQ = 4
PAST_LEN = 16384
PAGE_SIZE = 128

N_EVEN = (DEPTH + 1) // 2
N_ODD = DEPTH // 2

POOL_WINDOWS = (2, 4, 8, 16)
POOL_GROUPS = len(POOL_WINDOWS)
POOL_GROUP_DIM = D_MODEL // 16
POOL_DIM = POOL_GROUPS * POOL_GROUP_DIM
POOL_BUF = max(POOL_WINDOWS) - 1
V_HEAD = 128
QK_NOPE = 128
QK_ROPE = 64
QK_HEAD = QK_NOPE + QK_ROPE
MLA_HEADS = (D_MODEL - POOL_DIM) // V_HEAD
Q_LORA = D_MODEL // 4
KV_LORA = D_MODEL // 4
ROPE_THETA = 10000.0
MLA_SCALE = QK_HEAD ** -0.5
Q_BLOCK = 128
CCV_DIM = D_MODEL // 2
CCV_WIDTH = 31
SCONV_DIM = D_MODEL // 2
SCONV_WIDTH = 3
D_FF = 4 * D_MODEL
EPS = 1e-6

IN_AB = POOL_DIM + Q_LORA + KV_LORA + QK_ROPE
MIX_AB = POOL_DIM + MLA_HEADS * V_HEAD
IN_CD = 2 * CCV_DIM + 3 * SCONV_DIM
MIX_CD = CCV_DIM + SCONV_DIM

kernel_name = "hybrid_pool_mla_conv_decoder_step"


def rms_norm(x, g):
    xf = x.astype(jnp.float32)
    y = xf * lax.rsqrt(jnp.mean(xf * xf, axis=-1, keepdims=True) + EPS)
    return (y * g.astype(jnp.float32)).astype(x.dtype)


def layer_norm(x, g, b):
    xf = x.astype(jnp.float32)
    mu = jnp.mean(xf, axis=-1, keepdims=True)
    xc = xf - mu
    var = jnp.mean(xc * xc, axis=-1, keepdims=True)
    return (xc * lax.rsqrt(var + EPS) * g.astype(jnp.float32) + b.astype(jnp.float32)).astype(x.dtype)


def rope(x, pos):
    half = QK_ROPE // 2
    inv = ROPE_THETA ** (-jnp.arange(half, dtype=jnp.float32) / half)
    ang = pos.astype(jnp.float32)[:, None] * inv[None, :]
    ang = ang.reshape(ang.shape[:1] + (1,) * (x.ndim - 3) + (half,))
    cos, sin = jnp.cos(ang), jnp.sin(ang)
    xf = x.astype(jnp.float32)
    x1, x2 = xf[..., :half], xf[..., half:]
    return jnp.concatenate([x1 * cos - x2 * sin, x2 * cos + x1 * sin], axis=-1).astype(x.dtype)


def causal_dwconv(u, buf, w):
    ext = jnp.concatenate([buf.astype(u.dtype), u], axis=1)
    k = w.shape[0]
    out = lax.conv_general_dilated(ext, w[:, None, :].astype(u.dtype), window_strides=(1,), padding='VALID',
                                   dimension_numbers=('NWC', 'WIO', 'NWC'), feature_group_count=u.shape[-1])
    return out, ext[:, ext.shape[1] - (k - 1):]


def pool_mixer(u, buf, pos, w_grp, scale):
    n, t, _ = u.shape
    ext = jnp.concatenate([buf.astype(u.dtype), u], axis=1)
    cs = jnp.pad(jnp.cumsum(ext.astype(jnp.float32), axis=1), ((0, 0), (1, 0), (0, 0)))
    end = cs[:, POOL_BUF + 1:]
    outs = []
    for g, w in enumerate(POOL_WINDOWS):
        sl = slice(g * POOL_GROUP_DIM, (g + 1) * POOL_GROUP_DIM)
        start = cs[:, POOL_BUF + 1 - w: POOL_BUF + 1 - w + t, sl]
        cnt = jnp.minimum(pos + 1, w).astype(jnp.float32)[None, :, None]
        outs.append((end[..., sl] - start) / cnt)
    pooled = jnp.concatenate(outs, axis=-1) - u.astype(jnp.float32)
    pooled = pooled.reshape(n, t, POOL_GROUPS, POOL_GROUP_DIM)
    mixed = jnp.einsum('ntgc,gcd->ntgd', pooled, w_grp.astype(jnp.float32)).reshape(n, t, POOL_DIM)
    return (mixed * scale.astype(jnp.float32)).astype(u.dtype), ext[:, -POOL_BUF:]


def mla_prompt_attention(q_nope, q_rope, lat, k_rope, w_uk, w_uv):
    n, s = lat.shape[:2]
    k_nope = jnp.einsum('nsc,chd->nshd', lat, w_uk)
    v = jnp.einsum('nsc,chd->nshd', lat, w_uv)
    nb = s // Q_BLOCK
    qn = q_nope.reshape(n, nb, Q_BLOCK, MLA_HEADS, QK_NOPE).transpose(1, 0, 2, 3, 4)
    qr = q_rope.reshape(n, nb, Q_BLOCK, MLA_HEADS, QK_ROPE).transpose(1, 0, 2, 3, 4)
    kpos = jnp.arange(s)

    def block(args):
        qn_b, qr_b, b = args
        sc = (jnp.einsum('nqhd,nkhd->nhqk', qn_b, k_nope, preferred_element_type=jnp.float32)
              + jnp.einsum('nqhr,nkr->nhqk', qr_b, k_rope, preferred_element_type=jnp.float32)) * MLA_SCALE
        qpos = b * Q_BLOCK + jnp.arange(Q_BLOCK)
        sc = jnp.where(kpos[None, :] <= qpos[:, None], sc, -jnp.inf)
        p = jax.nn.softmax(sc, axis=-1)
        return jnp.einsum('nhqk,nkhd->nqhd', p.astype(v.dtype), v)

    o = lax.map(block, (qn, qr, jnp.arange(nb)))
    return o.transpose(1, 0, 2, 3, 4).reshape(n, s, MLA_HEADS * V_HEAD)


def mla_sample_attention(q_nope, q_rope, lat, k_rope, cache_lat, cache_kr, layer, page_table, w_uk, w_uv):
    n, t = lat.shape[:2]
    q_lat = jnp.einsum('nthd,chd->nthc', q_nope, w_uk)
    causal = jnp.arange(t)[None, :] <= jnp.arange(t)[:, None]

    def one_seq(args):
        ql, qr, lt, kr, pages = args
        past_lat = cache_lat[layer, pages].reshape(-1, KV_LORA)
        past_kr = cache_kr[layer, pages].reshape(-1, QK_ROPE)
        s_past = (jnp.einsum('thc,kc->htk', ql, past_lat, preferred_element_type=jnp.float32)
                  + jnp.einsum('thr,kr->htk', qr, past_kr, preferred_element_type=jnp.float32)) * MLA_SCALE
        s_new = (jnp.einsum('thc,kc->htk', ql, lt, preferred_element_type=jnp.float32)
                 + jnp.einsum('thr,kr->htk', qr, kr, preferred_element_type=jnp.float32)) * MLA_SCALE
        s_new = jnp.where(causal[None], s_new, -jnp.inf)
        p = jax.nn.softmax(jnp.concatenate([s_past, s_new], axis=-1), axis=-1).astype(lt.dtype)
        np_ = past_lat.shape[0]
        return (jnp.einsum('htk,kc->thc', p[..., :np_], past_lat)
                + jnp.einsum('htk,kc->thc', p[..., np_:], lt))

    o_lat = lax.map(one_seq, (q_lat, q_rope, lat, k_rope, page_table))
    return jnp.einsum('nthc,chd->nthd', o_lat, w_uv).reshape(n, t, MLA_HEADS * V_HEAD)


def mixer_ab(h, pos, pool_buf, w_in, pool_w, pool_scale, q_norm_g, w_uq, kv_norm_g, w_uk, w_uv, w_out, paged=None):
    z = h @ w_in
    o1 = POOL_DIM
    o2 = o1 + Q_LORA
    o3 = o2 + KV_LORA
    u, c_q, c_kv, k_r = z[..., :o1], z[..., o1:o2], z[..., o2:o3], z[..., o3:]
    a_out, new_pool = pool_mixer(u, pool_buf, pos, pool_w, pool_scale)
    q = jnp.einsum('ntq,qhd->nthd', rms_norm(c_q, q_norm_g), w_uq)
    q_nope, q_rope = q[..., :QK_NOPE], rope(q[..., QK_NOPE:], pos)
    lat = rms_norm(c_kv, kv_norm_g)
    k_rope = rope(k_r, pos)
    if paged is None:
        b_out = mla_prompt_attention(q_nope, q_rope, lat, k_rope, w_uk, w_uv)
    else:
        cache_lat, cache_kr, layer, page_table = paged
        b_out = mla_sample_attention(q_nope, q_rope, lat, k_rope, cache_lat, cache_kr, layer, page_table, w_uk, w_uv)
    y = jnp.concatenate([a_out, b_out.astype(a_out.dtype)], axis=-1) @ w_out
    return y, lat, k_rope, new_pool


def mixer_cd(h, ccv_buf, sconv_buf, w_in, ccv_w, ccv_b, ln_g, ln_b, sconv_w, w_out):
    z = h @ w_in
    val, gate = z[..., :CCV_DIM], z[..., CCV_DIM:2 * CCV_DIM]
    glu = val * jax.nn.sigmoid(gate)
    cv, new_ccv = causal_dwconv(glu, ccv_buf, ccv_w)
    c_out = jax.nn.silu(layer_norm(cv + ccv_b, ln_g, ln_b))
    o = 2 * CCV_DIM
    b_g = z[..., o:o + SCONV_DIM]
    c_g = z[..., o + SCONV_DIM:o + 2 * SCONV_DIM]
    h_in = z[..., o + 2 * SCONV_DIM:]
    sc, new_sconv = causal_dwconv(c_g * h_in, sconv_buf, sconv_w)
    d_out = b_g * sc
    y = jnp.concatenate([c_out, d_out], axis=-1) @ w_out
    return y, new_ccv, new_sconv


def sq_relu_mlp(h, w_up, w_down):
    a = jax.nn.relu(h @ w_up)
    return (a * a) @ w_down


def setup_inputs(seed: int = 0) -> dict:
    key = jax.random.key(seed)
    ks = iter(jax.random.split(key, 48))

    def nrm(shape, fan_in):
        return jax.random.normal(next(ks), shape, jnp.float32) * (fan_in ** -0.5)

    def gain(shape):
        return 1.0 + 0.05 * jax.random.normal(next(ks), shape, jnp.float32)

    def small(shape):
        return 0.02 * jax.random.normal(next(ks), shape, jnp.float32)

    n_pages = PAST_LEN // PAGE_SIZE
    n_used = DEC_BATCH * n_pages
    n_phys = n_used + n_used // 4
    page_table = jax.random.permutation(next(ks), n_phys)[:n_used].reshape(DEC_BATCH, n_pages).astype(jnp.int32)
    return {
        "x_prompt": jax.random.normal(next(ks), (BATCH, SEQ, D_MODEL), jnp.float32),
        "x_sample": jax.random.normal(next(ks), (DEC_BATCH, DEC_SEQ, D_MODEL), jnp.float32),
        "cache_mla_latent": nrm((N_EVEN, n_phys, PAGE_SIZE, KV_LORA), 1),
        "cache_mla_krope": nrm((N_EVEN, n_phys, PAGE_SIZE, QK_ROPE), 1),
        "page_table": page_table,
        "state_pool": nrm((N_EVEN, DEC_BATCH, POOL_BUF, POOL_DIM), 1),
        "state_ccv": nrm((N_ODD, DEC_BATCH, CCV_WIDTH - 1, CCV_DIM), 4),
        "state_sconv": nrm((N_ODD, DEC_BATCH, SCONV_WIDTH - 1, SCONV_DIM), 4),
        "norm1_g": gain((DEPTH, D_MODEL)),
        "norm2_g": gain((DEPTH, D_MODEL)),
        "final_norm_g": gain((D_MODEL,)),
        "w_in_ab": nrm((N_EVEN, D_MODEL, IN_AB), D_MODEL),
        "pool_w": nrm((N_EVEN, POOL_GROUPS, POOL_GROUP_DIM, POOL_GROUP_DIM), POOL_GROUP_DIM),
        "pool_scale": gain((N_EVEN, POOL_DIM)),
        "q_norm_g": gain((N_EVEN, Q_LORA)),
        "w_uq": nrm((N_EVEN, Q_LORA, MLA_HEADS, QK_HEAD), Q_LORA),
        "kv_norm_g": gain((N_EVEN, KV_LORA)),
        "w_uk": nrm((N_EVEN, KV_LORA, MLA_HEADS, QK_NOPE), KV_LORA),
        "w_uv": nrm((N_EVEN, KV_LORA, MLA_HEADS, V_HEAD), KV_LORA),
        "w_out_ab": nrm((N_EVEN, MIX_AB, D_MODEL), MIX_AB),
        "w_in_cd": nrm((N_ODD, D_MODEL, IN_CD), D_MODEL),
        "ccv_w": nrm((N_ODD, CCV_WIDTH, CCV_DIM), CCV_WIDTH),
        "ccv_b": small((N_ODD, CCV_DIM)),
        "ccv_ln_g": gain((N_ODD, CCV_DIM)),
        "ccv_ln_b": small((N_ODD, CCV_DIM)),
        "sconv_w": nrm((N_ODD, SCONV_WIDTH, SCONV_DIM), SCONV_WIDTH),
        "w_out_cd": nrm((N_ODD, MIX_CD, D_MODEL), MIX_CD),
        "w_up": nrm((DEPTH, D_MODEL, D_FF), D_MODEL),
        "w_down": nrm((DEPTH, D_FF, D_MODEL), D_FF),
    }


def reference(x_prompt, x_sample, cache_mla_latent, cache_mla_krope, page_table, state_pool, state_ccv, state_sconv,
              norm1_g, norm2_g, final_norm_g, w_in_ab, pool_w, pool_scale, q_norm_g, w_uq, kv_norm_g, w_uk, w_uv,
              w_out_ab, w_in_cd, ccv_w, ccv_b, ccv_ln_g, ccv_ln_b, sconv_w, w_out_cd, w_up, w_down):
    bsz, seq, _ = x_prompt.shape
    dseq = x_sample.shape[1]
    past_len = page_table.shape[1] * cache_mla_latent.shape[2]
    pos_p = jnp.arange(seq, dtype=jnp.int32)
    pos_s = past_len + jnp.arange(dseq, dtype=jnp.int32)
    dt = x_prompt.dtype
    zero_pool = jnp.zeros((bsz, POOL_BUF, POOL_DIM), dt)
    zero_ccv = jnp.zeros((bsz, CCV_WIDTH - 1, CCV_DIM), dt)
    zero_sconv = jnp.zeros((bsz, SCONV_WIDTH - 1, SCONV_DIM), dt)

    hp, hs = x_prompt, x_sample
    p_lat, p_kr, s_lat, s_kr, p_pool, s_pool = [], [], [], [], [], []
    p_ccv, s_ccv, p_sc, s_sc = [], [], [], []
    for layer in range(DEPTH):
        if layer % 2 == 0:
            i = layer // 2
            ew = (w_in_ab[i], pool_w[i], pool_scale[i], q_norm_g[i], w_uq[i], kv_norm_g[i], w_uk[i], w_uv[i], w_out_ab[i])
            yp, lat, kr, pool = mixer_ab(rms_norm(hp, norm1_g[layer]), pos_p, zero_pool, *ew)
            p_lat.append(lat); p_kr.append(kr); p_pool.append(pool)
            ys, lat, kr, pool = mixer_ab(rms_norm(hs, norm1_g[layer]), pos_s, state_pool[i], *ew,
                                         paged=(cache_mla_latent, cache_mla_krope, i, page_table))
            s_lat.append(lat); s_kr.append(kr); s_pool.append(pool)
        else:
            i = layer // 2
            ow = (w_in_cd[i], ccv_w[i], ccv_b[i], ccv_ln_g[i], ccv_ln_b[i], sconv_w[i], w_out_cd[i])
            yp, cc, sc = mixer_cd(rms_norm(hp, norm1_g[layer]), zero_ccv, zero_sconv, *ow)
            p_ccv.append(cc); p_sc.append(sc)
            ys, cc, sc = mixer_cd(rms_norm(hs, norm1_g[layer]), state_ccv[i], state_sconv[i], *ow)
            s_ccv.append(cc); s_sc.append(sc)
        hp = hp + yp
        hs = hs + ys
        hp = hp + sq_relu_mlp(rms_norm(hp, norm2_g[layer]), w_up[layer], w_down[layer])
        hs = hs + sq_relu_mlp(rms_norm(hs, norm2_g[layer]), w_up[layer], w_down[layer])

    y_prompt = rms_norm(hp, final_norm_g)
    y_sample = rms_norm(hs, final_norm_g)
    return (y_prompt, y_sample,
            jnp.stack(p_lat), jnp.stack(p_kr), jnp.stack(s_lat), jnp.stack(s_kr),
            jnp.stack(p_pool), jnp.stack(s_pool),
            jnp.stack(p_ccv), jnp.stack(s_ccv),
            jnp.stack(p_sc), jnp.stack(s_sc))
```

```python
import functools

import jax
import jax.numpy as jnp
from jax import lax
from jax.experimental import pallas as pl
from jax.experimental.pallas import tpu as pltpu

F32 = jnp.float32
BF16 = jnp.bfloat16

EPS = 1e-6
POOL_WINDOWS = (2, 4, 8, 16)
POOL_BUF = max(POOL_WINDOWS) - 1
LANES = 128
HEAD_PAD = 256
QK_NOPE = 128
QK_ROPE = 64
V_HEAD = 128
ROPE_THETA = 10000.0
MLA_SCALE = (QK_NOPE + QK_ROPE) ** -0.5
MIB = 1024 * 1024


def _params(semantics, vmem_mib):
    return pltpu.CompilerParams(dimension_semantics=semantics, vmem_limit_bytes=vmem_mib * MIB)


def _resident(shape, index_map):
    return pl.BlockSpec(shape, index_map, pipeline_mode=pl.Buffered(1))


def _rms(x, g):
    ms = jnp.mean(x * x, axis=-1, keepdims=True)
    return x * lax.rsqrt(ms + EPS) * g


def _dot(a, b):
    return jnp.dot(a, b, preferred_element_type=F32)


def _dot_nt(a, b):
    return lax.dot_general(a, b, (((1,), (1,)), ((), ())), preferred_element_type=F32)


def _inproj_ab_kernel(x_ref, g1_ref, win_ref, qg_ref, kvg_ref, wq_ref, wqr_ref, cos_ref, sin_ref, *rest,
                      heads, sample):
    if sample:
        wukt_ref, u_ref, lat_ref, kr_ref, qlat_ref, qrope_ref = rest
    else:
        wuk_ref, wuv_ref, u_ref, lat_ref, kr_ref, q_ref, k_ref, v_ref = rest
    pd = u_ref.shape[1]
    ql = qg_ref.shape[1]
    kvl = kvg_ref.shape[1]
    h = _rms(x_ref[...], g1_ref[...]).astype(BF16)
    z = _dot(h, win_ref[...])
    o1, o2, o3 = pd, pd + ql, pd + ql + kvl
    u_ref[...] = z[:, :o1]
    lat = _rms(z[:, o2:o3], kvg_ref[...])
    lat_ref[...] = lat
    cos = cos_ref[...]
    sin = sin_ref[...]
    krope = z[:, o3:o3 + LANES] * cos + z[:, o3 + LANES:o3 + 2 * LANES] * sin
    kr_ref[...] = krope[:, :QK_ROPE]
    cqn = _rms(z[:, o1:o2], qg_ref[...]).astype(BF16)
    qm = _dot(cqn, wq_ref[...])
    qrot = _dot(cqn, wqr_ref[...])
    if not sample:
        latb = lat.astype(BF16)
        kn = _dot(latb, wuk_ref[...])
        v_ref[...] = _dot(latb, wuv_ref[...]).astype(BF16)
        krb = krope.astype(BF16)
    for hd in range(heads):
        c0 = hd * HEAD_PAD
        nope = qm[:, c0:c0 + QK_NOPE] * MLA_SCALE
        rope = (qm[:, c0 + QK_NOPE:c0 + HEAD_PAD] * cos + qrot[:, hd * LANES:(hd + 1) * LANES] * sin) * MLA_SCALE
        if sample:
            qlat_ref[:, hd * kvl:(hd + 1) * kvl] = _dot(nope.astype(BF16), wukt_ref[hd]).astype(BF16)
            qrope_ref[:, hd * LANES:(hd + 1) * LANES] = rope
        else:
            q_ref[:, c0:c0 + QK_NOPE] = nope.astype(BF16)
            q_ref[:, c0 + QK_NOPE:c0 + HEAD_PAD] = rope.astype(BF16)
            k_ref[:, c0:c0 + QK_NOPE] = kn[:, hd * QK_NOPE:(hd + 1) * QK_NOPE].astype(BF16)
            k_ref[:, c0 + QK_NOPE:c0 + HEAD_PAD] = krb


def _inproj_ab(x, g1, win, qg, kvg, wq, wqr, cos, sin, extra_w, *, heads, sample, tm, table_period):
    n, d = x.shape
    ql, kvl = qg.shape[1], kvg.shape[1]
    pd = win.shape[1] - ql - kvl - 2 * LANES
    nt = table_period // tm
    row = lambda i: (i, 0)
    const = lambda i: (0, 0)
    in_specs = [
        pl.BlockSpec((tm, d), row),
        _resident((1, d), const),
        _resident(win.shape, const),
        _resident((1, ql), const),
        _resident((1, kvl), const),
        _resident(wq.shape, const),
        _resident(wqr.shape, const),
        pl.BlockSpec((tm, LANES), lambda i: (i % nt, 0)),
        pl.BlockSpec((tm, LANES), lambda i: (i % nt, 0)),
    ] + [_resident(w.shape, (lambda i: (0, 0, 0)) if w.ndim == 3 else const) for w in extra_w]
    out_shape = [jax.ShapeDtypeStruct((n, pd), F32), jax.ShapeDtypeStruct((n, kvl), F32),
                 jax.ShapeDtypeStruct((n, QK_ROPE), F32)]
    out_specs = [pl.BlockSpec((tm, pd), row), pl.BlockSpec((tm, kvl), row), pl.BlockSpec((tm, QK_ROPE), row)]
    if sample:
        out_shape += [jax.ShapeDtypeStruct((n, heads * kvl), BF16), jax.ShapeDtypeStruct((n, heads * LANES), F32)]
        out_specs += [pl.BlockSpec((tm, heads * kvl), row), pl.BlockSpec((tm, heads * LANES), row)]
    else:
        out_shape += [jax.ShapeDtypeStruct((n, heads * HEAD_PAD), BF16), jax.ShapeDtypeStruct((n, heads * HEAD_PAD), BF16),
                      jax.ShapeDtypeStruct((n, heads * V_HEAD), BF16)]
        out_specs += [pl.BlockSpec((tm, heads * HEAD_PAD), row), pl.BlockSpec((tm, heads * HEAD_PAD), row),
                      pl.BlockSpec((tm, heads * V_HEAD), row)]
    return pl.pallas_call(
        functools.partial(_inproj_ab_kernel, heads=heads, sample=sample),
        grid=(n // tm,), in_specs=in_specs, out_specs=out_specs, out_shape=out_shape,
        compiler_params=_params(("arbitrary",), 52),
        name="inproj_ab_sample" if sample else "inproj_ab_prompt",
    )(x, g1, win, qg, kvg, wq, wqr, cos, sin, *extra_w)


def _pool_mix(pooled, pw_ref, ps_ref):
    outs = []
    for g in range(len(POOL_WINDOWS)):
        sl = slice(g * LANES, (g + 1) * LANES)
        outs.append(_dot(pooled[:, sl].astype(BF16), pw_ref[g]))
    return jnp.concatenate(outs, axis=-1) * ps_ref[...]


def _pool_prompt_kernel(u_ref, pw_ref, ps_ref, o_ref, ext_ref, *, seq, tc):
    halo = 16
    ext_ref[0:halo, :] = jnp.zeros((halo, ext_ref.shape[1]), F32)
    ext_ref[halo:halo + seq, :] = u_ref[...]
    for r0 in range(0, seq, tc):
        pos = r0 + lax.broadcasted_iota(jnp.int32, (tc, 1), 0)
        cols = []
        for g, w in enumerate(POOL_WINDOWS):
            sl = slice(g * LANES, (g + 1) * LANES)
            cur = ext_ref[halo + r0:halo + r0 + tc, sl]
            s = cur
            for j in range(1, w):
                s = s + ext_ref[halo + r0 - j:halo + r0 - j + tc, sl]
            cnt = jnp.minimum(pos + 1, w).astype(F32)
            cols.append(s / cnt - cur)
        pooled = jnp.concatenate(cols, axis=-1)
        o_ref[r0:r0 + tc, :] = _pool_mix(pooled, pw_ref, ps_ref).astype(o_ref.dtype)


def _pool_prompt(u, pw, ps, *, seq):
    n, pd = u.shape
    return pl.pallas_call(
        functools.partial(_pool_prompt_kernel, seq=seq, tc=256),
        grid=(n // seq,),
        in_specs=[pl.BlockSpec((seq, pd), lambda b: (b, 0)),
                  _resident(pw.shape, lambda b: (0, 0, 0)),
                  _resident((1, pd), lambda b: (0, 0))],
        out_specs=pl.BlockSpec((seq, pd), lambda b: (b, 0)),
        out_shape=jax.ShapeDtypeStruct((n, pd), BF16),
        scratch_shapes=[pltpu.VMEM((seq + 16, pd), F32)],
        compiler_params=_params(("arbitrary",), 32),
        name="pool_prompt",
    )(u, pw, ps)


def _pool_sample_kernel(st_ref, u_ref, pw_ref, ps_ref, o_ref, *, dec_seq, pos0):
    nb = st_ref.shape[0]

    def row(r):
        return (lambda sl: st_ref[r, :, sl]) if r < nb else (lambda sl: u_ref[r - nb, :, sl])

    for t in range(dec_seq):
        cols = []
        for g, w in enumerate(POOL_WINDOWS):
            sl = slice(g * LANES, (g + 1) * LANES)
            cur = row(nb + t)(sl)
            s = cur
            for j in range(1, w):
                s = s + row(nb + t - j)(sl)
            cnt = float(min(pos0 + t + 1, w))
            cols.append(s / cnt - cur)
        pooled = jnp.concatenate(cols, axis=-1)
        o_ref[t] = _pool_mix(pooled, pw_ref, ps_ref).astype(o_ref.dtype)


def _pool_sample(st, u, pw, ps, *, pos0):
    t, b, pd = u.shape
    full3 = lambda i: (0, 0, 0)
    return pl.pallas_call(
        functools.partial(_pool_sample_kernel, dec_seq=t, pos0=pos0),
        grid=(1,),
        in_specs=[pl.BlockSpec(st.shape, full3), pl.BlockSpec(u.shape, full3),
                  pl.BlockSpec(pw.shape, full3), pl.BlockSpec((1, pd), lambda i: (0, 0))],
        out_specs=pl.BlockSpec((t, b, pd), full3),
        out_shape=jax.ShapeDtypeStruct((t, b, pd), BF16),
        compiler_params=_params(("arbitrary",), 32),
        name="pool_sample",
    )(st, u, pw, ps)


def _attn_prompt_kernel(q_ref, k_ref, v_ref, o_ref, *, tq, tk, heads):
    qi = pl.program_id(1)
    nkv = (qi * tq + tq + tk - 1) // tk
    row = qi * tq + lax.broadcasted_iota(jnp.int32, (tq, tk), 0)
    col = lax.broadcasted_iota(jnp.int32, (tq, tk), 1)
    for hd in range(heads):
        q = q_ref[:, hd * HEAD_PAD:(hd + 1) * HEAD_PAD]

        def body(j, carry, hd=hd, q=q):
            m, l, acc = carry
            k0 = pl.multiple_of(j * tk, tk)
            kk = k_ref[pl.ds(k0, tk), hd * HEAD_PAD:(hd + 1) * HEAD_PAD]
            s = _dot_nt(q, kk)
            s = jnp.where(col + k0 <= row, s, -jnp.inf)
            m_new = jnp.maximum(m, jnp.max(s, axis=-1, keepdims=True))
            alpha = jnp.exp(m - m_new)
            p = jnp.exp(s - m_new)
            l = alpha * l + jnp.sum(p, axis=-1, keepdims=True)
            vv = v_ref[pl.ds(k0, tk), hd * V_HEAD:(hd + 1) * V_HEAD]
            acc = alpha * acc + _dot(p.astype(BF16), vv)
            return m_new, l, acc

        init = (jnp.full((tq, 1), -jnp.inf, F32), jnp.zeros((tq, 1), F32), jnp.zeros((tq, V_HEAD), F32))
        _, l, acc = lax.fori_loop(0, nkv, body, init)
        o_ref[:, hd * V_HEAD:(hd + 1) * V_HEAD] = (acc / l).astype(o_ref.dtype)


def _attn_prompt(q, k, v, *, seq, heads, tq=256, tk=512):
    n = q.shape[0]
    nq = seq // tq
    return pl.pallas_call(
        functools.partial(_attn_prompt_kernel, tq=tq, tk=tk, heads=heads),
        grid=(n // seq, nq),
        in_specs=[pl.BlockSpec((tq, heads * HEAD_PAD), lambda b, i: (b * nq + i, 0)),
                  _resident((seq, heads * HEAD_PAD), lambda b, i: (b, 0)),
                  _resident((seq, heads * V_HEAD), lambda b, i: (b, 0))],
        out_specs=pl.BlockSpec((tq, heads * V_HEAD), lambda b, i: (b * nq + i, 0)),
        out_shape=jax.ShapeDtypeStruct((n, heads * V_HEAD), BF16),
        compiler_params=_params(("arbitrary", "arbitrary"), 48),
        name="attn_prompt",
    )(q, k, v)


def _attn_sample_kernel(pt_ref, ql_ref, qr_ref, latn_ref, krn_ref, clat_hbm, ckr_hbm, o_ref,
                        latbuf, krbuf, newlat, newkr, sem, *, pages_per_chunk, n_chunks, page, dec_seq, heads):
    s = pl.program_id(0)
    ns = pl.num_programs(0)
    ppc = pages_per_chunk

    def chunk_copies(seq, c, slot):
        cps = []
        for p in range(ppc):
            pg = pt_ref[seq, c * ppc + p]
            cps.append(pltpu.make_async_copy(clat_hbm.at[pg], latbuf.at[slot, pl.ds(p * page, page)], sem.at[slot, 0]))
            cps.append(pltpu.make_async_copy(ckr_hbm.at[pg], krbuf.at[slot, pl.ds(p * page, page)], sem.at[slot, 1]))
        return cps

    @pl.when(s == 0)
    def _():
        for cp in chunk_copies(0, 0, 0):
            cp.start()

    ql = ql_ref[0].astype(F32)
    qr = qr_ref[0]
    rows = ql.shape[0]

    newlat[...] = jnp.zeros(newlat.shape, F32)
    newkr[...] = jnp.zeros(newkr.shape, F32)
    newlat[0:dec_seq, :] = latn_ref[0]
    newkr[0:dec_seq, :] = krn_ref[0]
    nl = newlat[...]
    sc = _dot_nt(ql, nl) + _dot_nt(qr, newkr[...])
    r_i = lax.broadcasted_iota(jnp.int32, sc.shape, 0)
    j_i = lax.broadcasted_iota(jnp.int32, sc.shape, 1)
    sc = jnp.where(j_i * heads <= r_i, sc, -jnp.inf)
    m0 = jnp.max(sc, axis=-1, keepdims=True)
    p0 = jnp.exp(sc - m0)
    l0 = jnp.sum(p0, axis=-1, keepdims=True)
    acc0 = _dot(p0, nl)

    def body(c, carry):
        m, l, acc = carry
        slot = c % 2
        more = c + 1 < n_chunks
        nseq = jnp.where(more, s, s + 1)
        nc = jnp.where(more, c + 1, 0)

        @pl.when(jnp.logical_or(more, s + 1 < ns))
        def _():
            for cp in chunk_copies(nseq, nc, 1 - slot):
                cp.start()

        for cp in chunk_copies(s, c, slot):
            cp.wait()
        lat = latbuf[slot]
        sc = _dot_nt(ql, lat) + _dot_nt(qr, krbuf[slot])
        m_new = jnp.maximum(m, jnp.max(sc, axis=-1, keepdims=True))
        alpha = jnp.exp(m - m_new)
        p = jnp.exp(sc - m_new)
        l = alpha * l + jnp.sum(p, axis=-1, keepdims=True)
        acc = alpha * acc + _dot(p, lat)
        return m_new, l, acc

    _, l, acc = lax.fori_loop(0, n_chunks, body, (m0, l0, acc0))
    o_ref[0] = (acc / l).astype(o_ref.dtype)


def _attn_sample(page_table, ql, qr, latn, krn, cache_lat, cache_kr, *, heads, pages_per_chunk=8):
    nseq, n_pages = page_table.shape
    _, page, kvl = cache_lat.shape
    rope = cache_kr.shape[2]
    rows = ql.shape[1]
    dec_seq = latn.shape[1]
    assert n_pages % (2 * pages_per_chunk) == 0
    n_chunks = n_pages // pages_per_chunk
    chunk = pages_per_chunk * page
    per_seq = lambda s, pt: (s, 0, 0)
    grid_spec = pltpu.PrefetchScalarGridSpec(
        num_scalar_prefetch=1,
        grid=(nseq,),
        in_specs=[pl.BlockSpec((1, rows, kvl), per_seq),
                  pl.BlockSpec((1, rows, rope), per_seq),
                  pl.BlockSpec((1, dec_seq, kvl), per_seq),
                  pl.BlockSpec((1, dec_seq, rope), per_seq),
                  pl.BlockSpec(memory_space=pl.ANY),
                  pl.BlockSpec(memory_space=pl.ANY)],
        out_specs=pl.BlockSpec((1, rows, kvl), per_seq),
        scratch_shapes=[pltpu.VMEM((2, chunk, kvl), F32), pltpu.VMEM((2, chunk, rope), F32),
                        pltpu.VMEM((LANES, kvl), F32), pltpu.VMEM((LANES, rope), F32),
                        pltpu.SemaphoreType.DMA((2, 2))],
    )
    return pl.pallas_call(
        functools.partial(_attn_sample_kernel, pages_per_chunk=pages_per_chunk, n_chunks=n_chunks, page=page,
                          dec_seq=dec_seq, heads=heads),
        grid_spec=grid_spec,
        out_shape=jax.ShapeDtypeStruct((nseq, rows, kvl), BF16),
        compiler_params=_params(("arbitrary",), 32),
        name="attn_sample",
    )(page_table, ql, qr, latn, krn, cache_lat, cache_kr)


def _unabsorb_kernel(o_ref, wuv_ref, b_ref, *, heads):
    kvl = wuv_ref.shape[1]
    for hd in range(heads):
        b_ref[:, hd * V_HEAD:(hd + 1) * V_HEAD] = _dot(o_ref[:, hd * kvl:(hd + 1) * kvl], wuv_ref[hd]).astype(b_ref.dtype)


def _unabsorb(o_lat, wuv_h, *, heads, tm):
    n = o_lat.shape[0]
    return pl.pallas_call(
        functools.partial(_unabsorb_kernel, heads=heads),
        grid=(n // tm,),
        in_specs=[pl.BlockSpec((tm, o_lat.shape[1]), lambda i: (i, 0)),
                  _resident(wuv_h.shape, lambda i: (0, 0, 0))],
        out_specs=pl.BlockSpec((tm, heads * V_HEAD), lambda i: (i, 0)),
        out_shape=jax.ShapeDtypeStruct((n, heads * V_HEAD), BF16),
        compiler_params=_params(("arbitrary",), 32),
        name="unabsorb",
    )(o_lat, wuv_h)


def _outproj_kernel(*refs, widths):
    x_ref = refs[0]
    parts = refs[1:1 + len(widths)]
    w_ref = refs[1 + len(widths)]
    o_ref = refs[2 + len(widths)]
    acc = x_ref[...]
    off = 0
    for pr, w in zip(parts, widths):
        acc = acc + _dot(pr[...], w_ref[off:off + w, :])
        off += w
    o_ref[...] = acc


def _outproj(x, parts, w, *, tm):
    n, d = x.shape
    widths = tuple(p.shape[1] for p in parts)
    row = lambda i: (i, 0)
    return pl.pallas_call(
        functools.partial(_outproj_kernel, widths=widths),
        grid=(n // tm,),
        in_specs=[pl.BlockSpec((tm, d), row)] + [pl.BlockSpec((tm, wd), row) for wd in widths]
                 + [_resident(w.shape, lambda i: (0, 0))],
        out_specs=pl.BlockSpec((tm, d), row),
        out_shape=jax.ShapeDtypeStruct((n, d), F32),
        compiler_params=_params(("arbitrary",), 40),
        name="outproj",
    )(x, *parts, w)


def _mlp_kernel(x_ref, g_ref, wup_ref, wdn_ref, fg_ref, o_ref, xn_ref, *, final):
    j = pl.program_id(1)

    @pl.when(j == 0)
    def _():
        x = x_ref[...]
        xn_ref[...] = _rms(x, g_ref[...]).astype(BF16)
        o_ref[...] = x

    a = jnp.maximum(_dot(xn_ref[...], wup_ref[...]), 0.0)
    o_ref[...] += _dot((a * a).astype(BF16), wdn_ref[...])

    if final:
        @pl.when(j == pl.num_programs(1) - 1)
        def _():
            o_ref[...] = _rms(o_ref[...], fg_ref[...])


def _mlp(x, g, wup, wdn, fg, *, final, tm, tf):
    n, d = x.shape
    dff = wup.shape[1]
    return pl.pallas_call(
        functools.partial(_mlp_kernel, final=final),
        grid=(n // tm, dff // tf),
        in_specs=[pl.BlockSpec((tm, d), lambda i, j: (i, 0)),
                  _resident((1, d), lambda i, j: (0, 0)),
                  pl.BlockSpec((d, tf), lambda i, j: (0, j)),
                  pl.BlockSpec((tf, d), lambda i, j: (j, 0)),
                  _resident((1, d), lambda i, j: (0, 0))],
        out_specs=pl.BlockSpec((tm, d), lambda i, j: (i, 0)),
        out_shape=jax.ShapeDtypeStruct((n, d), F32),
        scratch_shapes=[pltpu.VMEM((tm, d), BF16)],
        compiler_params=_params(("arbitrary", "arbitrary"), 48),
        name="mlp_final" if final else "mlp",
    )(x, g, wup, wdn, fg)


def _inproj_cd_kernel(x_ref, g_ref, w_ref, glu_ref, cgh_ref, bg_ref, xn_ref):
    j = pl.program_id(1)

    @pl.when(j == 0)
    def _():
        xn_ref[...] = _rms(x_ref[...], g_ref[...]).astype(BF16)

    tw = glu_ref.shape[1]
    z = _dot(xn_ref[...], w_ref[...])
    glu_ref[...] = z[:, 0:tw] * jax.nn.sigmoid(z[:, tw:2 * tw])
    bg_ref[...] = z[:, 2 * tw:3 * tw]
    cgh_ref[...] = z[:, 3 * tw:4 * tw] * z[:, 4 * tw:5 * tw]


def _inproj_cd(x, g, w, *, tm, tw):
    n, d = x.shape
    cdim = w.shape[1] // 5
    out = jax.ShapeDtypeStruct((n, cdim), F32)
    ospec = pl.BlockSpec((tm, tw), lambda i, j: (i, j))
    return pl.pallas_call(
        _inproj_cd_kernel,
        grid=(n // tm, cdim // tw),
        in_specs=[pl.BlockSpec((tm, d), lambda i, j: (i, 0)),
                  _resident((1, d), lambda i, j: (0, 0)),
                  pl.BlockSpec((d, 5 * tw), lambda i, j: (0, j))],
        out_specs=[ospec, ospec, ospec],
        out_shape=[out, out, out],
        scratch_shapes=[pltpu.VMEM((tm, d), BF16)],
        compiler_params=_params(("arbitrary", "arbitrary"), 40),
        name="inproj_cd",
    )(x, g, w)


def _ln_silu(y, g, b):
    mu = jnp.mean(y, axis=-1, keepdims=True)
    yc = y - mu
    var = jnp.mean(yc * yc, axis=-1, keepdims=True)
    z = yc * lax.rsqrt(var + EPS) * g + b
    return z * jax.nn.sigmoid(z)


def _conv_prompt_kernel(glu_ref, cgh_ref, bg_ref, cw_ref, cb_ref, lg_ref, lb_ref, sw_ref, o_ref,
                        ext_ref, ext2_ref, y_ref, *, tt, rc, lt):
    tb = pl.program_id(1)
    kw = cw_ref.shape[0]
    sk = sw_ref.shape[0]
    cdim = glu_ref.shape[1]
    halo, halo2 = 32, 8

    @pl.when(tb == 0)
    def _():
        ext_ref[0:halo, :] = jnp.zeros((halo, cdim), F32)
        ext2_ref[0:halo2, :] = jnp.zeros((halo2, cdim), F32)

    ext_ref[halo:halo + tt, :] = glu_ref[...]
    ext2_ref[halo2:halo2 + tt, :] = cgh_ref[...]

    for l0 in range(0, cdim, lt):
        ls = slice(l0, l0 + lt)
        for r0 in range(0, tt, rc):
            acc = jnp.zeros((rc, lt), F32)
            for k in range(kw):
                base = halo - (kw - 1) + r0 + k
                acc = acc + ext_ref[base:base + rc, ls] * cw_ref[k:k + 1, ls]
            y_ref[r0:r0 + rc, ls] = acc + cb_ref[:, ls]
            sc = jnp.zeros((rc, lt), F32)
            for k in range(sk):
                base = halo2 - (sk - 1) + r0 + k
                sc = sc + ext2_ref[base:base + rc, ls] * sw_ref[k:k + 1, ls]
            o_ref[r0:r0 + rc, cdim + l0:cdim + l0 + lt] = (bg_ref[r0:r0 + rc, ls] * sc).astype(o_ref.dtype)

    o_ref[:, 0:cdim] = _ln_silu(y_ref[...], lg_ref[...], lb_ref[...]).astype(o_ref.dtype)

    ext_ref[0:halo, :] = ext_ref[tt:tt + halo, :]
    ext2_ref[0:halo2, :] = ext2_ref[tt:tt + halo2, :]


def _conv_prompt(glu, cgh, bg, cw, cb, lg, lb, sw, *, seq, tt=256):
    n, cdim = glu.shape
    nt = seq // tt
    blk = lambda b, t: (b * nt + t, 0)
    c2 = lambda b, t: (0, 0)
    return pl.pallas_call(
        functools.partial(_conv_prompt_kernel, tt=tt, rc=32, lt=256),
        grid=(n // seq, nt),
        in_specs=[pl.BlockSpec((tt, cdim), blk), pl.BlockSpec((tt, cdim), blk), pl.BlockSpec((tt, cdim), blk),
                  _resident(cw.shape, c2), _resident((1, cdim), c2), _resident((1, cdim), c2),
                  _resident((1, cdim), c2), _resident(sw.shape, c2)],
        out_specs=pl.BlockSpec((tt, 2 * cdim), blk),
        out_shape=jax.ShapeDtypeStruct((n, 2 * cdim), BF16),
        scratch_shapes=[pltpu.VMEM((tt + 32, cdim), F32), pltpu.VMEM((tt + 8, cdim), F32), pltpu.VMEM((tt, cdim), F32)],
        compiler_params=_params(("arbitrary", "arbitrary"), 32),
        name="conv_prompt",
    )(glu, cgh, bg, cw, cb, lg, lb, sw)


def _conv_sample_kernel(cst_ref, sst_ref, glu_ref, cgh_ref, bg_ref, cw_ref, cb_ref, lg_ref, lb_ref, sw_ref, o_ref,
                        *, dec_seq):
    kw = cw_ref.shape[0]
    sk = sw_ref.shape[0]
    cdim = glu_ref.shape[2]

    def ext(r):
        return cst_ref[r] if r < kw - 1 else glu_ref[r - (kw - 1)]

    def ext2(r):
        return sst_ref[r] if r < sk - 1 else cgh_ref[r - (sk - 1)]

    for t in range(dec_seq):
        acc = ext(t) * cw_ref[0:1, :]
        for k in range(1, kw):
            acc = acc + ext(t + k) * cw_ref[k:k + 1, :]
        o_ref[t, :, 0:cdim] = _ln_silu(acc + cb_ref[...], lg_ref[...], lb_ref[...]).astype(o_ref.dtype)
        sc = ext2(t) * sw_ref[0:1, :]
        for k in range(1, sk):
            sc = sc + ext2(t + k) * sw_ref[k:k + 1, :]
        o_ref[t, :, cdim:2 * cdim] = (bg_ref[t] * sc).astype(o_ref.dtype)


def _conv_sample(cst, sst, glu, cgh, bg, cw, cb, lg, lb, sw, *, tb=32):
    t, b, cdim = glu.shape
    blk = lambda i: (0, i, 0)
    c2 = lambda i: (0, 0)
    new = pl.BlockSpec((t, tb, cdim), blk)
    return pl.pallas_call(
        functools.partial(_conv_sample_kernel, dec_seq=t),
        grid=(b // tb,),
        in_specs=[pl.BlockSpec((cst.shape[0], tb, cdim), blk), pl.BlockSpec((sst.shape[0], tb, cdim), blk),
                  new, new, new,
                  _resident(cw.shape, c2), _resident((1, cdim), c2), _resident((1, cdim), c2),
                  _resident((1, cdim), c2), _resident(sw.shape, c2)],
        out_specs=pl.BlockSpec((t, tb, 2 * cdim), blk),
        out_shape=jax.ShapeDtypeStruct((t, b, 2 * cdim), BF16),
        compiler_params=_params(("arbitrary",), 32),
        name="conv_sample",
    )(cst, sst, glu, cgh, bg, cw, cb, lg, lb, sw)


def _rope_tables(pos):
    half = QK_ROPE // 2
    inv = ROPE_THETA ** (-jnp.arange(half, dtype=F32) / half)
    ang = pos.astype(F32)[:, None] * inv[None, :]
    z = jnp.zeros((pos.shape[0], LANES - QK_ROPE), F32)
    cos, sin = jnp.cos(ang), jnp.sin(ang)
    return jnp.concatenate([cos, cos, z], axis=-1), jnp.concatenate([sin, sin, z], axis=-1)


def _rot_cols(w):
    half = QK_ROPE // 2
    return jnp.concatenate([-w[..., half:], w[..., :half]], axis=-1)


def _pad_cols(w, width):
    return jnp.concatenate([w, jnp.zeros(w.shape[:-1] + (width - w.shape[-1],), w.dtype)], axis=-1)


def _tail(buf, new, keep):
    new_tail = new[:, max(0, new.shape[1] - keep):]
    return jnp.concatenate([buf.astype(new.dtype), new_tail], axis=1)[:, -keep:]


def kernel(x_prompt, x_sample, cache_mla_latent, cache_mla_krope, page_table, state_pool, state_ccv, state_sconv,
           norm1_g, norm2_g, final_norm_g, w_in_ab, pool_w, pool_scale, q_norm_g, w_uq, kv_norm_g, w_uk, w_uv,
           w_out_ab, w_in_cd, ccv_w, ccv_b, ccv_ln_g, ccv_ln_b, sconv_w, w_out_cd, w_up, w_down):
    bsz, seq, d = x_prompt.shape
    dbsz, dseq, _ = x_sample.shape
    depth = norm1_g.shape[0]
    page = cache_mla_latent.shape[2]
    past_len = page_table.shape[1] * page
    heads = w_uq.shape[2]
    pd = state_pool.shape[-1]
    ql = q_norm_g.shape[-1]
    kvl = kv_norm_g.shape[-1]
    cdim = state_ccv.shape[-1]
    n_p, n_s = bsz * seq, dbsz * dseq
    tm_p = 256 if n_p % 256 == 0 else n_p
    tm_s = 256 if n_s % 256 == 0 else n_s
    tmm_p = 512 if n_p % 512 == 0 else tm_p
    tmm_s = 512 if n_s % 512 == 0 else tm_s

    hp = x_prompt.reshape(n_p, d)
    hs = x_sample.transpose(1, 0, 2).reshape(n_s, d)

    cos_p, sin_p = _rope_tables(jnp.arange(seq, dtype=jnp.int32))
    pos_s = jnp.repeat(past_len + jnp.arange(dseq, dtype=jnp.int32), dbsz)
    cos_s, sin_s = _rope_tables(pos_s)

    def row2(v):
        return v.reshape(1, -1)

    def to_bt(a):
        return a.reshape(dseq, dbsz, -1).transpose(1, 0, 2)

    p_lat, p_kr, s_lat, s_kr, p_pool, s_pool = [], [], [], [], [], []
    p_ccv, s_ccv, p_sc, s_sc = [], [], [], []
    for layer in range(depth):
        i = layer // 2
        last = layer == depth - 1
        if layer % 2 == 0:
            wi = w_in_ab[i]
            o3 = pd + ql + kvl
            k_r = wi[:, o3:]
            win = jnp.concatenate([wi[:, :o3], _pad_cols(k_r, LANES), _pad_cols(_rot_cols(k_r), LANES)], axis=-1).astype(BF16)
            wq_h = w_uq[i]
            wq = _pad_cols(wq_h, HEAD_PAD).reshape(ql, heads * HEAD_PAD).astype(BF16)
            wqr = _pad_cols(_rot_cols(wq_h[..., QK_NOPE:]), LANES).reshape(ql, heads * LANES).astype(BF16)
            wuk2 = w_uk[i].reshape(kvl, heads * QK_NOPE).astype(BF16)
            wuv2 = w_uv[i].reshape(kvl, heads * V_HEAD).astype(BF16)
            wuk_t = w_uk[i].transpose(1, 2, 0).astype(BF16)
            wuv_h = w_uv[i].transpose(1, 0, 2).astype(BF16)
            pw = pool_w[i].astype(BF16)
            ps = row2(pool_scale[i])
            wo = w_out_ab[i].astype(BF16)
            common = (row2(norm1_g[layer]), win, row2(q_norm_g[i]), row2(kv_norm_g[i]), wq, wqr)

            u, lat, kr, q, k, v = _inproj_ab(hp, *common, cos_p, sin_p, (wuk2, wuv2), heads=heads, sample=False,
                                             tm=tm_p, table_period=seq)
            a_out = _pool_prompt(u, pw, ps, seq=seq)
            b_out = _attn_prompt(q, k, v, seq=seq, heads=heads)
            hp = _outproj(hp, [a_out, b_out], wo, tm=tmm_p)
            p_lat.append(lat.reshape(bsz, seq, kvl))
            p_kr.append(kr.reshape(bsz, seq, QK_ROPE))
            p_pool.append(_tail(jnp.zeros((bsz, POOL_BUF, pd), F32), u.reshape(bsz, seq, pd), POOL_BUF))

            u, lat, kr, qlat, qrope = _inproj_ab(hs, *common, cos_s, sin_s, (wuk_t,), heads=heads, sample=True,
                                                 tm=tm_s, table_period=n_s)
            a_out = _pool_sample(state_pool[i].transpose(1, 0, 2), u.reshape(dseq, dbsz, pd), pw, ps, pos0=past_len)
            ql_b = qlat.reshape(dseq, dbsz, heads, kvl).transpose(1, 0, 2, 3).reshape(dbsz, dseq * heads, kvl)
            qr_b = (qrope.reshape(dseq, dbsz, heads, LANES)[..., :QK_ROPE]
                    .transpose(1, 0, 2, 3).reshape(dbsz, dseq * heads, QK_ROPE))
            lat_b, kr_b = to_bt(lat), to_bt(kr)
            o_lat = _attn_sample(page_table, ql_b, qr_b, lat_b, kr_b, cache_mla_latent[i], cache_mla_krope[i],
                                 heads=heads)
            o_tm = o_lat.reshape(dbsz, dseq, heads * kvl).transpose(1, 0, 2).reshape(n_s, heads * kvl)
            b_out = _unabsorb(o_tm, wuv_h, heads=heads, tm=tm_s)
            hs = _outproj(hs, [a_out.reshape(n_s, pd), b_out], wo, tm=tmm_s)
            s_lat.append(lat_b)
            s_kr.append(kr_b)
            s_pool.append(_tail(state_pool[i], to_bt(u), POOL_BUF))
        else:
            tw = 256
            wcd = w_in_cd[i]
            wcd = (wcd.reshape(d, 5, cdim // tw, tw).transpose(0, 2, 1, 3).reshape(d, 5 * cdim)).astype(BF16)
            conv_w = (ccv_w[i], row2(ccv_b[i]), row2(ccv_ln_g[i]), row2(ccv_ln_b[i]), sconv_w[i])
            wo = w_out_cd[i].astype(BF16)
            kw, sk = ccv_w.shape[1], sconv_w.shape[1]

            glu, cgh, bg = _inproj_cd(hp, row2(norm1_g[layer]), wcd, tm=tmm_p, tw=tw)
            cd = _conv_prompt(glu, cgh, bg, *conv_w, seq=seq)
            hp = _outproj(hp, [cd], wo, tm=tmm_p)
            p_ccv.append(_tail(jnp.zeros((bsz, kw - 1, cdim), F32), glu.reshape(bsz, seq, cdim), kw - 1))
            p_sc.append(_tail(jnp.zeros((bsz, sk - 1, cdim), F32), cgh.reshape(bsz, seq, cdim), sk - 1))

            glu, cgh, bg = _inproj_cd(hs, row2(norm1_g[layer]), wcd, tm=tmm_s, tw=tw)
            cd = _conv_sample(state_ccv[i].transpose(1, 0, 2), state_sconv[i].transpose(1, 0, 2),
                              glu.reshape(dseq, dbsz, cdim), cgh.reshape(dseq, dbsz, cdim),
                              bg.reshape(dseq, dbsz, cdim), *conv_w)
            hs = _outproj(hs, [cd.reshape(n_s, 2 * cdim)], wo, tm=tmm_s)
            s_ccv.append(_tail(state_ccv[i], to_bt(glu), kw - 1))
            s_sc.append(_tail(state_sconv[i], to_bt(cgh), sk - 1))

        wup = w_up[layer].astype(BF16)
        wdn = w_down[layer].astype(BF16)
        fg = row2(final_norm_g)
        hp = _mlp(hp, row2(norm2_g[layer]), wup, wdn, fg, final=last, tm=tmm_p, tf=1024)
        hs = _mlp(hs, row2(norm2_g[layer]), wup, wdn, fg, final=last, tm=tmm_s, tf=1024)

    y_prompt = hp.reshape(bsz, seq, d)
    y_sample = to_bt(hs)
    return (y_prompt, y_sample,
            jnp.stack(p_lat), jnp.stack(p_kr), jnp.stack(s_lat), jnp.stack(s_kr),
            jnp.stack(p_pool), jnp.stack(s_pool),
            jnp.stack(p_ccv), jnp.stack(s_ccv),
            jnp.stack(p_sc), jnp.stack(s_sc))
```

```python
import functools

import jax
import jax.numpy as jnp
from jax import lax
from jax.experimental import pallas as pl
from jax.experimental.pallas import tpu as pltpu

F32 = jnp.float32
BF16 = jnp.bfloat16

EPS = 1e-6
POOL_WINDOWS = (2, 4, 8, 16)
POOL_BUF = max(POOL_WINDOWS) - 1
LANES = 128
HEAD_PAD = 256
QK_NOPE = 128
QK_ROPE = 64
V_HEAD = 128
ROPE_THETA = 10000.0
MLA_SCALE = (QK_NOPE + QK_ROPE) ** -0.5
MIB = 1024 * 1024


def _params(semantics, vmem_mib):
    return pltpu.CompilerParams(dimension_semantics=semantics, vmem_limit_bytes=vmem_mib * MIB)


def _resident(shape, index_map):
    return pl.BlockSpec(shape, index_map, pipeline_mode=pl.Buffered(1))


def _rms(x, g):
    ms = jnp.mean(x * x, axis=-1, keepdims=True)
    return x * lax.rsqrt(ms + EPS) * g


def _dot(a, b):
    return jnp.dot(a, b, preferred_element_type=F32)


def _dot_nt(a, b):
    return lax.dot_general(a, b, (((1,), (1,)), ((), ())), preferred_element_type=F32)


def _inproj_ab_kernel(x_ref, g1_ref, win_ref, qg_ref, kvg_ref, wq_ref, wqr_ref, cos_ref, sin_ref, *rest,
                      heads, sample):
    if sample:
        wukt_ref, u_ref, lat_ref, kr_ref, qlat_ref, qrope_ref = rest
    else:
        wuk_ref, wuv_ref, u_ref, lat_ref, kr_ref, q_ref, k_ref, v_ref = rest
    pd = u_ref.shape[1]
    ql = qg_ref.shape[1]
    kvl = kvg_ref.shape[1]
    h = _rms(x_ref[...], g1_ref[...]).astype(BF16)
    z = _dot(h, win_ref[...])
    o1, o2, o3 = pd, pd + ql, pd + ql + kvl
    u_ref[...] = z[:, :o1]
    lat = _rms(z[:, o2:o3], kvg_ref[...])
    lat_ref[...] = lat
    cos = cos_ref[...]
    sin = sin_ref[...]
    krope = z[:, o3:o3 + LANES] * cos + z[:, o3 + LANES:o3 + 2 * LANES] * sin
    kr_ref[...] = krope[:, :QK_ROPE]
    cqn = _rms(z[:, o1:o2], qg_ref[...]).astype(BF16)
    qm = _dot(cqn, wq_ref[...])
    qrot = _dot(cqn, wqr_ref[...])
    if not sample:
        latb = lat.astype(BF16)
        kn = _dot(latb, wuk_ref[...])
        v_ref[...] = _dot(latb, wuv_ref[...]).astype(BF16)
        krb = krope.astype(BF16)
    for hd in range(heads):
        c0 = hd * HEAD_PAD
        nope = qm[:, c0:c0 + QK_NOPE] * MLA_SCALE
        rope = (qm[:, c0 + QK_NOPE:c0 + HEAD_PAD] * cos + qrot[:, hd * LANES:(hd + 1) * LANES] * sin) * MLA_SCALE
        if sample:
            qlat_ref[:, hd * kvl:(hd + 1) * kvl] = _dot(nope.astype(BF16), wukt_ref[hd]).astype(BF16)
            qrope_ref[:, hd * LANES:(hd + 1) * LANES] = rope
        else:
            q_ref[:, c0:c0 + QK_NOPE] = nope.astype(BF16)
            q_ref[:, c0 + QK_NOPE:c0 + HEAD_PAD] = rope.astype(BF16)
            k_ref[:, c0:c0 + QK_NOPE] = kn[:, hd * QK_NOPE:(hd + 1) * QK_NOPE].astype(BF16)
            k_ref[:, c0 + QK_NOPE:c0 + HEAD_PAD] = krb


def _inproj_ab(x, g1, win, qg, kvg, wq, wqr, cos, sin, extra_w, *, heads, sample, tm, table_period):
    n, d = x.shape
    ql, kvl = qg.shape[1], kvg.shape[1]
    pd = win.shape[1] - ql - kvl - 2 * LANES
    nt = table_period // tm
    row = lambda i: (i, 0)
    const = lambda i: (0, 0)
    in_specs = [
        pl.BlockSpec((tm, d), row),
        _resident((1, d), const),
        _resident(win.shape, const),
        _resident((1, ql), const),
        _resident((1, kvl), const),
        _resident(wq.shape, const),
        _resident(wqr.shape, const),
        pl.BlockSpec((tm, LANES), lambda i: (i % nt, 0)),
        pl.BlockSpec((tm, LANES), lambda i: (i % nt, 0)),
    ] + [_resident(w.shape, (lambda i: (0, 0, 0)) if w.ndim == 3 else const) for w in extra_w]
    out_shape = [jax.ShapeDtypeStruct((n, pd), F32), jax.ShapeDtypeStruct((n, kvl), F32),
                 jax.ShapeDtypeStruct((n, QK_ROPE), F32)]
    out_specs = [pl.BlockSpec((tm, pd), row), pl.BlockSpec((tm, kvl), row), pl.BlockSpec((tm, QK_ROPE), row)]
    if sample:
        out_shape += [jax.ShapeDtypeStruct((n, heads * kvl), BF16), jax.ShapeDtypeStruct((n, heads * LANES), F32)]
        out_specs += [pl.BlockSpec((tm, heads * kvl), row), pl.BlockSpec((tm, heads * LANES), row)]
    else:
        out_shape += [jax.ShapeDtypeStruct((n, heads * HEAD_PAD), BF16), jax.ShapeDtypeStruct((n, heads * HEAD_PAD), BF16),
                      jax.ShapeDtypeStruct((n, heads * V_HEAD), BF16)]
        out_specs += [pl.BlockSpec((tm, heads * HEAD_PAD), row), pl.BlockSpec((tm, heads * HEAD_PAD), row),
                      pl.BlockSpec((tm, heads * V_HEAD), row)]
    return pl.pallas_call(
        functools.partial(_inproj_ab_kernel, heads=heads, sample=sample),
        grid=(n // tm,), in_specs=in_specs, out_specs=out_specs, out_shape=out_shape,
        compiler_params=_params(("arbitrary",), 52),
        name="inproj_ab_sample" if sample else "inproj_ab_prompt",
    )(x, g1, win, qg, kvg, wq, wqr, cos, sin, *extra_w)


def _pool_mix(pooled, pw_ref, ps_ref):
    outs = []
    for g in range(len(POOL_WINDOWS)):
        sl = slice(g * LANES, (g + 1) * LANES)
        outs.append(_dot(pooled[:, sl].astype(BF16), pw_ref[g]))
    return jnp.concatenate(outs, axis=-1) * ps_ref[...]


def _pool_prompt_kernel(u_ref, pw_ref, ps_ref, o_ref, ext_ref, *, seq, tc):
    halo = 16
    ext_ref[0:halo, :] = jnp.zeros((halo, ext_ref.shape[1]), F32)
    ext_ref[halo:halo + seq, :] = u_ref[...]
    for r0 in range(0, seq, tc):
        pos = r0 + lax.broadcasted_iota(jnp.int32, (tc, 1), 0)
        cols = []
        for g, w in enumerate(POOL_WINDOWS):
            sl = slice(g * LANES, (g + 1) * LANES)
            cur = ext_ref[halo + r0:halo + r0 + tc, sl]
            s = cur
            for j in range(1, w):
                s = s + ext_ref[halo + r0 - j:halo + r0 - j + tc, sl]
            cnt = jnp.minimum(pos + 1, w).astype(F32)
            cols.append(s / cnt - cur)
        pooled = jnp.concatenate(cols, axis=-1)
        o_ref[r0:r0 + tc, :] = _pool_mix(pooled, pw_ref, ps_ref).astype(o_ref.dtype)


def _pool_prompt(u, pw, ps, *, seq):
    n, pd = u.shape
    return pl.pallas_call(
        functools.partial(_pool_prompt_kernel, seq=seq, tc=256),
        grid=(n // seq,),
        in_specs=[pl.BlockSpec((seq, pd), lambda b: (b, 0)),
                  _resident(pw.shape, lambda b: (0, 0, 0)),
                  _resident((1, pd), lambda b: (0, 0))],
        out_specs=pl.BlockSpec((seq, pd), lambda b: (b, 0)),
        out_shape=jax.ShapeDtypeStruct((n, pd), BF16),
        scratch_shapes=[pltpu.VMEM((seq + 16, pd), F32)],
        compiler_params=_params(("arbitrary",), 32),
        name="pool_prompt",
    )(u, pw, ps)


def _pool_sample_kernel(st_ref, u_ref, pw_ref, ps_ref, o_ref, *, dec_seq, pos0):
    nb = st_ref.shape[0]

    def row(r):
        return (lambda sl: st_ref[r, :, sl]) if r < nb else (lambda sl: u_ref[r - nb, :, sl])

    for t in range(dec_seq):
        cols = []
        for g, w in enumerate(POOL_WINDOWS):
            sl = slice(g * LANES, (g + 1) * LANES)
            cur = row(nb + t)(sl)
            s = cur
            for j in range(1, w):
                s = s + row(nb + t - j)(sl)
            cnt = float(min(pos0 + t + 1, w))
            cols.append(s / cnt - cur)
        pooled = jnp.concatenate(cols, axis=-1)
        o_ref[t] = _pool_mix(pooled, pw_ref, ps_ref).astype(o_ref.dtype)


def _pool_sample(st, u, pw, ps, *, pos0):
    t, b, pd = u.shape
    full3 = lambda i: (0, 0, 0)
    return pl.pallas_call(
        functools.partial(_pool_sample_kernel, dec_seq=t, pos0=pos0),
        grid=(1,),
        in_specs=[pl.BlockSpec(st.shape, full3), pl.BlockSpec(u.shape, full3),
                  pl.BlockSpec(pw.shape, full3), pl.BlockSpec((1, pd), lambda i: (0, 0))],
        out_specs=pl.BlockSpec((t, b, pd), full3),
        out_shape=jax.ShapeDtypeStruct((t, b, pd), BF16),
        compiler_params=_params(("arbitrary",), 32),
        name="pool_sample",
    )(st, u, pw, ps)


def _softmax_step(s, m, l, acc, vv):
    m_new = jnp.maximum(m, jnp.max(s, axis=-1, keepdims=True))
    alpha = jnp.exp(m - m_new)
    p = jnp.exp(s - m_new)
    l = alpha * l + jnp.sum(p, axis=-1, keepdims=True)
    acc = alpha * acc + _dot(p.astype(BF16), vv)
    return m_new, l, acc


def _attn_prompt_kernel(q_ref, k_ref, v_ref, o_ref, *, tq, heads, heads_per_pass):
    qi = pl.program_id(1)
    causal = (lax.broadcasted_iota(jnp.int32, (tq, tq), 1) <= lax.broadcasted_iota(jnp.int32, (tq, tq), 0))
    for h0 in range(0, heads, heads_per_pass):
        hds = tuple(range(h0, h0 + heads_per_pass))
        qs = [q_ref[:, hd * HEAD_PAD:(hd + 1) * HEAD_PAD] for hd in hds]

        def step(k0, carry, masked, hds=hds, qs=qs):
            new = []
            for q, hd, (m, l, acc) in zip(qs, hds, carry):
                s = _dot_nt(q, k_ref[pl.ds(k0, tq), hd * HEAD_PAD:(hd + 1) * HEAD_PAD])
                if masked:
                    s = jnp.where(causal, s, -jnp.inf)
                new.append(_softmax_step(s, m, l, acc, v_ref[pl.ds(k0, tq), hd * V_HEAD:(hd + 1) * V_HEAD]))
            return tuple(new)

        init = tuple((jnp.full((tq, 1), -jnp.inf, F32), jnp.zeros((tq, 1), F32), jnp.zeros((tq, V_HEAD), F32))
                     for _ in hds)
        carry = lax.fori_loop(0, qi, lambda j, c: step(pl.multiple_of(j * tq, tq), c, False), init)
        carry = step(pl.multiple_of(qi * tq, tq), carry, True)
        for hd, (_, l, acc) in zip(hds, carry):
            o_ref[:, hd * V_HEAD:(hd + 1) * V_HEAD] = (acc / l).astype(o_ref.dtype)


def _attn_prompt(q, k, v, *, seq, heads, tq=512, heads_per_pass=2):
    n = q.shape[0]
    tq = min(tq, seq)
    nq = seq // tq
    return pl.pallas_call(
        functools.partial(_attn_prompt_kernel, tq=tq, heads=heads, heads_per_pass=heads_per_pass),
        grid=(n // seq, nq),
        in_specs=[pl.BlockSpec((tq, heads * HEAD_PAD), lambda b, i: (b * nq + i, 0)),
                  _resident((seq, heads * HEAD_PAD), lambda b, i: (b, 0)),
                  _resident((seq, heads * V_HEAD), lambda b, i: (b, 0))],
        out_specs=pl.BlockSpec((tq, heads * V_HEAD), lambda b, i: (b * nq + i, 0)),
        out_shape=jax.ShapeDtypeStruct((n, heads * V_HEAD), BF16),
        compiler_params=_params(("arbitrary", "arbitrary"), 48),
        name="attn_prompt",
    )(q, k, v)


def _attn_sample_kernel(pt_ref, ql_ref, qr_ref, latn_ref, krn_ref, clat_hbm, ckrt_hbm, o_ref,
                        latbuf, krbuf, newlat, newkr, sem, *, pages_per_chunk, n_chunks, page, dec_seq, heads,
                        n_slots, n_streams):
    s = pl.program_id(0)
    total = pl.num_programs(0) * n_chunks
    ppc = pages_per_chunk
    ahead = n_slots - 1

    def chunk_copies(g, slot):
        seq = lax.div(g, n_chunks)
        c = lax.rem(g, n_chunks)
        cps = []
        for p in range(ppc):
            pg = pt_ref[seq, c * ppc + p]
            cps.append(pltpu.make_async_copy(clat_hbm.at[pg], latbuf.at[slot, pl.ds(p * page, page)], sem.at[slot, 0]))
            cps.append(pltpu.make_async_copy(ckrt_hbm.at[pg], krbuf.at[slot, :, pl.ds(p * page, page)], sem.at[slot, 1]))
        return cps

    @pl.when(s == 0)
    def _():
        for g0 in range(ahead):
            for cp in chunk_copies(jnp.int32(g0), g0):
                cp.start()

    ql = ql_ref[0]
    qr = qr_ref[0].astype(BF16)

    newlat[...] = jnp.zeros(newlat.shape, F32)
    newkr[...] = jnp.zeros(newkr.shape, F32)
    newlat[0:dec_seq, :] = latn_ref[0]
    newkr[0:dec_seq, :] = krn_ref[0]
    nl = newlat[...].astype(BF16)
    sc = _dot_nt(ql, nl) + _dot_nt(qr, newkr[...].astype(BF16))
    r_i = lax.broadcasted_iota(jnp.int32, sc.shape, 0)
    j_i = lax.broadcasted_iota(jnp.int32, sc.shape, 1)
    sc = jnp.where(j_i * heads <= r_i, sc, -jnp.inf)
    rows = sc.shape[0]
    empty = (jnp.full((rows, 1), -jnp.inf, F32), jnp.zeros((rows, 1), F32), jnp.zeros((rows, nl.shape[1]), F32))
    init = (_softmax_step(sc, *empty, nl),) + (empty,) * (n_streams - 1)
    part = ppc * page // n_streams

    def body(c, carry):
        g = s * n_chunks + c
        slot = lax.rem(g, n_slots)
        gn = g + ahead

        @pl.when(gn < total)
        def _():
            for cp in chunk_copies(gn, lax.rem(gn, n_slots)):
                cp.start()

        for cp in chunk_copies(g, slot):
            cp.wait()
        new = []
        for st, (m, l, acc) in enumerate(carry):
            lat = latbuf[slot, pl.ds(st * part, part), :].astype(BF16)
            krt = krbuf[slot, :, pl.ds(st * part, part)].astype(BF16)
            new.append(_softmax_step(_dot_nt(ql, lat) + _dot(qr, krt), m, l, acc, lat))
        return tuple(new)

    carry = lax.fori_loop(0, n_chunks, body, init)
    m = functools.reduce(jnp.maximum, [st[0] for st in carry])
    l = sum(st[1] * jnp.exp(st[0] - m) for st in carry)
    acc = sum(st[2] * jnp.exp(st[0] - m) for st in carry)
    o_ref[0] = (acc / l).astype(o_ref.dtype)


def _attn_sample(page_table, ql, qr, latn, krn, cache_lat, cache_krt, *, heads, n_slots=3, n_streams=2):
    nseq, n_pages = page_table.shape
    _, page, kvl = cache_lat.shape
    rope = cache_krt.shape[1]
    rows = ql.shape[1]
    dec_seq = latn.shape[1]
    pages_per_chunk = min(16, n_pages)
    assert n_pages % pages_per_chunk == 0 and nseq * (n_pages // pages_per_chunk) >= n_slots - 1
    n_chunks = n_pages // pages_per_chunk
    chunk = pages_per_chunk * page
    per_seq = lambda s, pt: (s, 0, 0)
    grid_spec = pltpu.PrefetchScalarGridSpec(
        num_scalar_prefetch=1,
        grid=(nseq,),
        in_specs=[pl.BlockSpec((1, rows, kvl), per_seq),
                  pl.BlockSpec((1, rows, rope), per_seq),
                  pl.BlockSpec((1, dec_seq, kvl), per_seq),
                  pl.BlockSpec((1, dec_seq, rope), per_seq),
                  pl.BlockSpec(memory_space=pl.ANY),
                  pl.BlockSpec(memory_space=pl.ANY)],
        out_specs=pl.BlockSpec((1, rows, kvl), per_seq),
        scratch_shapes=[pltpu.VMEM((n_slots, chunk, kvl), F32), pltpu.VMEM((n_slots, rope, chunk), F32),
                        pltpu.VMEM((LANES, kvl), F32), pltpu.VMEM((LANES, rope), F32),
                        pltpu.SemaphoreType.DMA((n_slots, 2))],
    )
    return pl.pallas_call(
        functools.partial(_attn_sample_kernel, pages_per_chunk=pages_per_chunk, n_chunks=n_chunks, page=page,
                          dec_seq=dec_seq, heads=heads, n_slots=n_slots, n_streams=n_streams),
        grid_spec=grid_spec,
        out_shape=jax.ShapeDtypeStruct((nseq, rows, kvl), BF16),
        compiler_params=_params(("arbitrary",), 44),
        name="attn_sample",
    )(page_table, ql, qr, latn, krn, cache_lat, cache_krt)


def _unabsorb_kernel(o_ref, wuv_ref, b_ref, *, heads):
    kvl = wuv_ref.shape[1]
    for hd in range(heads):
        b_ref[:, hd * V_HEAD:(hd + 1) * V_HEAD] = _dot(o_ref[:, hd * kvl:(hd + 1) * kvl], wuv_ref[hd]).astype(b_ref.dtype)


def _unabsorb(o_lat, wuv_h, *, heads, tm):
    n = o_lat.shape[0]
    return pl.pallas_call(
        functools.partial(_unabsorb_kernel, heads=heads),
        grid=(n // tm,),
        in_specs=[pl.BlockSpec((tm, o_lat.shape[1]), lambda i: (i, 0)),
                  _resident(wuv_h.shape, lambda i: (0, 0, 0))],
        out_specs=pl.BlockSpec((tm, heads * V_HEAD), lambda i: (i, 0)),
        out_shape=jax.ShapeDtypeStruct((n, heads * V_HEAD), BF16),
        compiler_params=_params(("arbitrary",), 32),
        name="unabsorb",
    )(o_lat, wuv_h)


def _outproj_kernel(*refs, widths):
    x_ref = refs[0]
    parts = refs[1:1 + len(widths)]
    w_ref = refs[1 + len(widths)]
    o_ref = refs[2 + len(widths)]
    acc = x_ref[...]
    off = 0
    for pr, w in zip(parts, widths):
        acc = acc + _dot(pr[...], w_ref[off:off + w, :])
        off += w
    o_ref[...] = acc


def _outproj(x, parts, w, *, tm):
    n, d = x.shape
    widths = tuple(p.shape[1] for p in parts)
    row = lambda i: (i, 0)
    return pl.pallas_call(
        functools.partial(_outproj_kernel, widths=widths),
        grid=(n // tm,),
        in_specs=[pl.BlockSpec((tm, d), row)] + [pl.BlockSpec((tm, wd), row) for wd in widths]
                 + [_resident(w.shape, lambda i: (0, 0))],
        out_specs=pl.BlockSpec((tm, d), row),
        out_shape=jax.ShapeDtypeStruct((n, d), F32),
        compiler_params=_params(("arbitrary",), 40),
        name="outproj",
    )(x, *parts, w)


def _mlp_kernel(x_ref, g_ref, wup_ref, wdn_ref, fg_ref, o_ref, xn_ref, *, final):
    j = pl.program_id(1)

    @pl.when(j == 0)
    def _():
        x = x_ref[...]
        xn_ref[...] = _rms(x, g_ref[...]).astype(BF16)
        o_ref[...] = x

    a = jnp.maximum(_dot(xn_ref[...], wup_ref[...]), 0.0)
    o_ref[...] += _dot((a * a).astype(BF16), wdn_ref[...])

    if final:
        @pl.when(j == pl.num_programs(1) - 1)
        def _():
            o_ref[...] = _rms(o_ref[...], fg_ref[...])


def _mlp(x, g, wup, wdn, fg, *, layer, final, tm, tf):
    n, d = x.shape
    dff = wup.shape[2]
    return pl.pallas_call(
        functools.partial(_mlp_kernel, final=final),
        grid=(n // tm, dff // tf),
        in_specs=[pl.BlockSpec((tm, d), lambda i, j: (i, 0)),
                  _resident((1, d), lambda i, j: (0, 0)),
                  pl.BlockSpec((None, d, tf), lambda i, j: (layer, 0, j)),
                  pl.BlockSpec((None, tf, d), lambda i, j: (layer, j, 0)),
                  _resident((1, d), lambda i, j: (0, 0))],
        out_specs=pl.BlockSpec((tm, d), lambda i, j: (i, 0)),
        out_shape=jax.ShapeDtypeStruct((n, d), F32),
        scratch_shapes=[pltpu.VMEM((tm, d), BF16)],
        compiler_params=_params(("arbitrary", "arbitrary"), 48),
        name="mlp_final" if final else "mlp",
    )(x, g, wup, wdn, fg)


def _inproj_cd_kernel(x_ref, g_ref, wv_ref, wg_ref, wb_ref, wc_ref, wh_ref, glu_ref, cgh_ref, bg_ref, xn_ref):
    j = pl.program_id(1)

    @pl.when(j == 0)
    def _():
        xn_ref[...] = _rms(x_ref[...], g_ref[...]).astype(BF16)

    xn = xn_ref[...]
    glu_ref[...] = _dot(xn, wv_ref[...]) * jax.nn.sigmoid(_dot(xn, wg_ref[...]))
    bg_ref[...] = _dot(xn, wb_ref[...])
    cgh_ref[...] = _dot(xn, wc_ref[...]) * _dot(xn, wh_ref[...])


def _inproj_cd(x, g, w, *, tm, tw):
    n, d = x.shape
    cdim = w.shape[1] // 5
    nct = cdim // tw
    out = jax.ShapeDtypeStruct((n, cdim), F32)
    ospec = pl.BlockSpec((tm, tw), lambda i, j: (i, j))
    return pl.pallas_call(
        _inproj_cd_kernel,
        grid=(n // tm, nct),
        in_specs=[pl.BlockSpec((tm, d), lambda i, j: (i, 0)),
                  _resident((1, d), lambda i, j: (0, 0))]
                 + [pl.BlockSpec((d, tw), lambda i, j, k=k: (0, k * nct + j)) for k in range(5)],
        out_specs=[ospec, ospec, ospec],
        out_shape=[out, out, out],
        scratch_shapes=[pltpu.VMEM((tm, d), BF16)],
        compiler_params=_params(("arbitrary", "arbitrary"), 40),
        name="inproj_cd",
    )(x, g, w, w, w, w, w)


def _ln_silu(y, g, b):
    mu = jnp.mean(y, axis=-1, keepdims=True)
    yc = y - mu
    var = jnp.mean(yc * yc, axis=-1, keepdims=True)
    z = yc * lax.rsqrt(var + EPS) * g + b
    return z * jax.nn.sigmoid(z)


def _conv_prompt_kernel(glu_ref, cgh_ref, bg_ref, cw_ref, cb_ref, lg_ref, lb_ref, sw_ref, o_ref,
                        ext_ref, sh_ref, ext2_ref, y_ref, *, tt, rc, lt):
    tb = pl.program_id(1)
    kw = cw_ref.shape[0]
    sk = sw_ref.shape[0]
    cdim = glu_ref.shape[1]
    sub = 8
    halo, halo2 = 32, 8

    @pl.when(tb == 0)
    def _():
        ext_ref[0:halo, :] = jnp.zeros((halo, cdim), F32)
        ext2_ref[0:halo2, :] = jnp.zeros((halo2, cdim), F32)

    ext_ref[halo:halo + tt, :] = glu_ref[...]
    ext2_ref[halo2:halo2 + tt, :] = cgh_ref[...]
    span = halo + tt - sub
    for j in range(1, sub):
        sh_ref[j - 1, 0:span, :] = ext_ref[j:j + span, :]

    for l0 in range(0, cdim, lt):
        ls = slice(l0, l0 + lt)
        for r0 in range(0, tt, rc):
            acc = jnp.zeros((rc, lt), F32)
            for k in range(kw):
                off = halo - (kw - 1) + k
                j, base = off % sub, r0 + off - off % sub
                rows = ext_ref[base:base + rc, ls] if j == 0 else sh_ref[j - 1, base:base + rc, ls]
                acc = acc + rows * cw_ref[k:k + 1, ls]
            y_ref[r0:r0 + rc, ls] = acc + cb_ref[:, ls]
            sc = jnp.zeros((rc, lt), F32)
            for k in range(sk):
                base = halo2 - (sk - 1) + r0 + k
                sc = sc + ext2_ref[base:base + rc, ls] * sw_ref[k:k + 1, ls]
            o_ref[r0:r0 + rc, cdim + l0:cdim + l0 + lt] = (bg_ref[r0:r0 + rc, ls] * sc).astype(o_ref.dtype)

    o_ref[:, 0:cdim] = _ln_silu(y_ref[...], lg_ref[...], lb_ref[...]).astype(o_ref.dtype)

    ext_ref[0:halo, :] = ext_ref[tt:tt + halo, :]
    ext2_ref[0:halo2, :] = ext2_ref[tt:tt + halo2, :]


def _conv_prompt(glu, cgh, bg, cw, cb, lg, lb, sw, *, seq, tt=256):
    n, cdim = glu.shape
    nt = seq // tt
    blk = lambda b, t: (b * nt + t, 0)
    c2 = lambda b, t: (0, 0)
    return pl.pallas_call(
        functools.partial(_conv_prompt_kernel, tt=tt, rc=tt, lt=LANES),
        grid=(n // seq, nt),
        in_specs=[pl.BlockSpec((tt, cdim), blk), pl.BlockSpec((tt, cdim), blk), pl.BlockSpec((tt, cdim), blk),
                  _resident(cw.shape, c2), _resident((1, cdim), c2), _resident((1, cdim), c2),
                  _resident((1, cdim), c2), _resident(sw.shape, c2)],
        out_specs=pl.BlockSpec((tt, 2 * cdim), blk),
        out_shape=jax.ShapeDtypeStruct((n, 2 * cdim), BF16),
        scratch_shapes=[pltpu.VMEM((tt + 32, cdim), F32), pltpu.VMEM((7, tt + 24, cdim), F32),
                        pltpu.VMEM((tt + 8, cdim), F32), pltpu.VMEM((tt, cdim), F32)],
        compiler_params=_params(("arbitrary", "arbitrary"), 40),
        name="conv_prompt",
    )(glu, cgh, bg, cw, cb, lg, lb, sw)


def _conv_sample_kernel(cst_ref, sst_ref, glu_ref, cgh_ref, bg_ref, cw_ref, cb_ref, lg_ref, lb_ref, sw_ref, o_ref,
                        *, dec_seq):
    kw = cw_ref.shape[0]
    sk = sw_ref.shape[0]
    cdim = glu_ref.shape[2]

    def ext(r):
        return cst_ref[r] if r < kw - 1 else glu_ref[r - (kw - 1)]

    def ext2(r):
        return sst_ref[r] if r < sk - 1 else cgh_ref[r - (sk - 1)]

    for t in range(dec_seq):
        acc = ext(t) * cw_ref[0:1, :]
        for k in range(1, kw):
            acc = acc + ext(t + k) * cw_ref[k:k + 1, :]
        o_ref[t, :, 0:cdim] = _ln_silu(acc + cb_ref[...], lg_ref[...], lb_ref[...]).astype(o_ref.dtype)
        sc = ext2(t) * sw_ref[0:1, :]
        for k in range(1, sk):
            sc = sc + ext2(t + k) * sw_ref[k:k + 1, :]
        o_ref[t, :, cdim:2 * cdim] = (bg_ref[t] * sc).astype(o_ref.dtype)


def _conv_sample(cst, sst, glu, cgh, bg, cw, cb, lg, lb, sw, *, tb=32):
    t, b, cdim = glu.shape
    blk = lambda i: (0, i, 0)
    c2 = lambda i: (0, 0)
    new = pl.BlockSpec((t, tb, cdim), blk)
    return pl.pallas_call(
        functools.partial(_conv_sample_kernel, dec_seq=t),
        grid=(b // tb,),
        in_specs=[pl.BlockSpec((cst.shape[0], tb, cdim), blk), pl.BlockSpec((sst.shape[0], tb, cdim), blk),
                  new, new, new,
                  _resident(cw.shape, c2), _resident((1, cdim), c2), _resident((1, cdim), c2),
                  _resident((1, cdim), c2), _resident(sw.shape, c2)],
        out_specs=pl.BlockSpec((t, tb, 2 * cdim), blk),
        out_shape=jax.ShapeDtypeStruct((t, b, 2 * cdim), BF16),
        compiler_params=_params(("arbitrary",), 32),
        name="conv_sample",
    )(cst, sst, glu, cgh, bg, cw, cb, lg, lb, sw)


def _rope_tables(pos):
    half = QK_ROPE // 2
    inv = ROPE_THETA ** (-jnp.arange(half, dtype=F32) / half)
    ang = pos.astype(F32)[:, None] * inv[None, :]
    z = jnp.zeros((pos.shape[0], LANES - QK_ROPE), F32)
    cos, sin = jnp.cos(ang), jnp.sin(ang)
    return jnp.concatenate([cos, cos, z], axis=-1), jnp.concatenate([sin, sin, z], axis=-1)


def _rot_cols(w):
    half = QK_ROPE // 2
    return jnp.concatenate([-w[..., half:], w[..., :half]], axis=-1)


def _pad_cols(w, width):
    return jnp.concatenate([w, jnp.zeros(w.shape[:-1] + (width - w.shape[-1],), w.dtype)], axis=-1)


def _tail(buf, new, keep):
    new_tail = new[:, max(0, new.shape[1] - keep):]
    return jnp.concatenate([buf.astype(new.dtype), new_tail], axis=1)[:, -keep:]


def kernel(x_prompt, x_sample, cache_mla_latent, cache_mla_krope, page_table, state_pool, state_ccv, state_sconv,
           norm1_g, norm2_g, final_norm_g, w_in_ab, pool_w, pool_scale, q_norm_g, w_uq, kv_norm_g, w_uk, w_uv,
           w_out_ab, w_in_cd, ccv_w, ccv_b, ccv_ln_g, ccv_ln_b, sconv_w, w_out_cd, w_up, w_down):
    bsz, seq, d = x_prompt.shape
    dbsz, dseq, _ = x_sample.shape
    depth = norm1_g.shape[0]
    page = cache_mla_latent.shape[2]
    past_len = page_table.shape[1] * page
    heads = w_uq.shape[2]
    pd = state_pool.shape[-1]
    ql = q_norm_g.shape[-1]
    kvl = kv_norm_g.shape[-1]
    cdim = state_ccv.shape[-1]
    n_p, n_s = bsz * seq, dbsz * dseq
    tm_p = 256 if n_p % 256 == 0 else n_p
    tm_s = 256 if n_s % 256 == 0 else n_s
    tmm_p = 512 if n_p % 512 == 0 else tm_p
    tmm_s = 512 if n_s % 512 == 0 else tm_s

    hp = x_prompt.reshape(n_p, d)
    hs = x_sample.transpose(1, 0, 2).reshape(n_s, d)

    cos_p, sin_p = _rope_tables(jnp.arange(seq, dtype=jnp.int32))
    pos_s = jnp.repeat(past_len + jnp.arange(dseq, dtype=jnp.int32), dbsz)
    cos_s, sin_s = _rope_tables(pos_s)

    def row2(v):
        return v.reshape(1, -1)

    def to_bt(a):
        return a.reshape(dseq, dbsz, -1).transpose(1, 0, 2)

    wup = w_up.astype(BF16)
    wdn = w_down.astype(BF16)
    fg = row2(final_norm_g)
    p_lat, p_kr, s_lat, s_kr, p_pool, s_pool = [], [], [], [], [], []
    p_ccv, s_ccv, p_sc, s_sc = [], [], [], []
    for layer in range(depth):
        i = layer // 2
        last = layer == depth - 1
        if layer % 2 == 0:
            wi = w_in_ab[i]
            o3 = pd + ql + kvl
            k_r = wi[:, o3:]
            win = jnp.concatenate([wi[:, :o3], _pad_cols(k_r, LANES), _pad_cols(_rot_cols(k_r), LANES)], axis=-1).astype(BF16)
            wq_h = w_uq[i]
            wq = _pad_cols(wq_h, HEAD_PAD).reshape(ql, heads * HEAD_PAD).astype(BF16)
            wqr = _pad_cols(_rot_cols(wq_h[..., QK_NOPE:]), LANES).reshape(ql, heads * LANES).astype(BF16)
            wuk2 = w_uk[i].reshape(kvl, heads * QK_NOPE).astype(BF16)
            wuv2 = w_uv[i].reshape(kvl, heads * V_HEAD).astype(BF16)
            wuk_t = w_uk[i].transpose(1, 2, 0).astype(BF16)
            wuv_h = w_uv[i].transpose(1, 0, 2).astype(BF16)
            pw = pool_w[i].astype(BF16)
            ps = row2(pool_scale[i])
            wo = w_out_ab[i].astype(BF16)
            common = (row2(norm1_g[layer]), win, row2(q_norm_g[i]), row2(kv_norm_g[i]), wq, wqr)

            u, lat, kr, q, k, v = _inproj_ab(hp, *common, cos_p, sin_p, (wuk2, wuv2), heads=heads, sample=False,
                                             tm=tm_p, table_period=seq)
            a_out = _pool_prompt(u, pw, ps, seq=seq)
            b_out = _attn_prompt(q, k, v, seq=seq, heads=heads)
            hp = _outproj(hp, [a_out, b_out], wo, tm=tmm_p)
            p_lat.append(lat.reshape(bsz, seq, kvl))
            p_kr.append(kr.reshape(bsz, seq, QK_ROPE))
            p_pool.append(_tail(jnp.zeros((bsz, POOL_BUF, pd), F32), u.reshape(bsz, seq, pd), POOL_BUF))

            u, lat, kr, qlat, qrope = _inproj_ab(hs, *common, cos_s, sin_s, (wuk_t,), heads=heads, sample=True,
                                                 tm=tm_s, table_period=n_s)
            a_out = _pool_sample(state_pool[i].transpose(1, 0, 2), u.reshape(dseq, dbsz, pd), pw, ps, pos0=past_len)
            ql_b = qlat.reshape(dseq, dbsz, heads, kvl).transpose(1, 0, 2, 3).reshape(dbsz, dseq * heads, kvl)
            qr_b = (qrope.reshape(dseq, dbsz, heads, LANES)[..., :QK_ROPE]
                    .transpose(1, 0, 2, 3).reshape(dbsz, dseq * heads, QK_ROPE))
            lat_b, kr_b = to_bt(lat), to_bt(kr)
            o_lat = _attn_sample(page_table, ql_b, qr_b, lat_b, kr_b, cache_mla_latent[i],
                                 jnp.swapaxes(cache_mla_krope[i], 1, 2), heads=heads)
            o_tm = o_lat.reshape(dbsz, dseq, heads * kvl).transpose(1, 0, 2).reshape(n_s, heads * kvl)
            b_out = _unabsorb(o_tm, wuv_h, heads=heads, tm=tm_s)
            hs = _outproj(hs, [a_out.reshape(n_s, pd), b_out], wo, tm=tmm_s)
            s_lat.append(lat_b)
            s_kr.append(kr_b)
            s_pool.append(_tail(state_pool[i], to_bt(u), POOL_BUF))
        else:
            tw = 256
            wcd = w_in_cd[i].astype(BF16)
            conv_w = (ccv_w[i], row2(ccv_b[i]), row2(ccv_ln_g[i]), row2(ccv_ln_b[i]), sconv_w[i])
            wo = w_out_cd[i].astype(BF16)
            kw, sk = ccv_w.shape[1], sconv_w.shape[1]

            glu, cgh, bg = _inproj_cd(hp, row2(norm1_g[layer]), wcd, tm=tmm_p, tw=tw)
            cd = _conv_prompt(glu, cgh, bg, *conv_w, seq=seq)
            hp = _outproj(hp, [cd], wo, tm=tmm_p)
            p_ccv.append(_tail(jnp.zeros((bsz, kw - 1, cdim), F32), glu.reshape(bsz, seq, cdim), kw - 1))
            p_sc.append(_tail(jnp.zeros((bsz, sk - 1, cdim), F32), cgh.reshape(bsz, seq, cdim), sk - 1))

            glu, cgh, bg = _inproj_cd(hs, row2(norm1_g[layer]), wcd, tm=tmm_s, tw=tw)
            cd = _conv_sample(state_ccv[i].transpose(1, 0, 2), state_sconv[i].transpose(1, 0, 2),
                              glu.reshape(dseq, dbsz, cdim), cgh.reshape(dseq, dbsz, cdim),
                              bg.reshape(dseq, dbsz, cdim), *conv_w)
            hs = _outproj(hs, [cd.reshape(n_s, 2 * cdim)], wo, tm=tmm_s)
            s_ccv.append(_tail(state_ccv[i], to_bt(glu), kw - 1))
            s_sc.append(_tail(state_sconv[i], to_bt(cgh), sk - 1))

        hp = _mlp(hp, row2(norm2_g[layer]), wup, wdn, fg, layer=layer, final=last, tm=tmm_p, tf=1024)
        hs = _mlp(hs, row2(norm2_g[layer]), wup, wdn, fg, layer=layer, final=last, tm=tmm_s, tf=1024)

    y_prompt = hp.reshape(bsz, seq, d)
    y_sample = to_bt(hs)
    return (y_prompt, y_sample,
            jnp.stack(p_lat), jnp.stack(p_kr), jnp.stack(s_lat), jnp.stack(s_kr),
            jnp.stack(p_pool), jnp.stack(s_pool),
            jnp.stack(p_ccv), jnp.stack(s_ccv),
            jnp.stack(p_sc), jnp.stack(s_sc))
```

```python
import functools

import jax
import jax.numpy as jnp
from jax import lax
from jax.experimental import pallas as pl
from jax.experimental.pallas import tpu as pltpu

F32 = jnp.float32
BF16 = jnp.bfloat16

EPS = 1e-6
POOL_WINDOWS = (2, 4, 8, 16)
POOL_BUF = max(POOL_WINDOWS) - 1
LANES = 128
HEAD_PAD = 256
QK_NOPE = 128
QK_ROPE = 64
V_HEAD = 128
ROPE_THETA = 10000.0
MLA_SCALE = (QK_NOPE + QK_ROPE) ** -0.5
MIB = 1024 * 1024


def _params(semantics, vmem_mib):
    return pltpu.CompilerParams(dimension_semantics=semantics, vmem_limit_bytes=vmem_mib * MIB)


def _resident(shape, index_map):
    return pl.BlockSpec(shape, index_map, pipeline_mode=pl.Buffered(1))


def _rms(x, g):
    ms = jnp.mean(x * x, axis=-1, keepdims=True)
    return x * lax.rsqrt(ms + EPS) * g


def _dot(a, b):
    return jnp.dot(a, b, preferred_element_type=F32)


def _dot_nt(a, b):
    return lax.dot_general(a, b, (((1,), (1,)), ((), ())), preferred_element_type=F32)


def _inproj_ab_kernel(x_ref, g1_ref, win_ref, qg_ref, kvg_ref, wq_ref, wqr_ref, cos_ref, sin_ref, *rest,
                      heads, sample):
    if sample:
        wukt_ref, u_ref, lat_ref, kr_ref, qlat_ref, qrope_ref = rest
    else:
        wuk_ref, wuv_ref, u_ref, lat_ref, kr_ref, q_ref, k_ref, v_ref = rest
    pd = u_ref.shape[1]
    ql = qg_ref.shape[1]
    kvl = kvg_ref.shape[1]
    h = _rms(x_ref[...], g1_ref[...]).astype(BF16)
    z = _dot(h, win_ref[...])
    o1, o2, o3 = pd, pd + ql, pd + ql + kvl
    u_ref[...] = z[:, :o1]
    lat = _rms(z[:, o2:o3], kvg_ref[...])
    lat_ref[...] = lat
    cos = cos_ref[...]
    sin = sin_ref[...]
    krope = z[:, o3:o3 + LANES] * cos + z[:, o3 + LANES:o3 + 2 * LANES] * sin
    kr_ref[...] = krope[:, :QK_ROPE]
    cqn = _rms(z[:, o1:o2], qg_ref[...]).astype(BF16)
    qm = _dot(cqn, wq_ref[...])
    qrot = _dot(cqn, wqr_ref[...])
    if not sample:
        latb = lat.astype(BF16)
        kn = _dot(latb, wuk_ref[...])
        v_ref[...] = _dot(latb, wuv_ref[...]).astype(BF16)
        krb = krope.astype(BF16)
    for hd in range(heads):
        c0 = hd * HEAD_PAD
        nope = qm[:, c0:c0 + QK_NOPE] * MLA_SCALE
        rope = (qm[:, c0 + QK_NOPE:c0 + HEAD_PAD] * cos + qrot[:, hd * LANES:(hd + 1) * LANES] * sin) * MLA_SCALE
        if sample:
            qlat_ref[:, hd * kvl:(hd + 1) * kvl] = _dot(nope.astype(BF16), wukt_ref[hd]).astype(BF16)
            qrope_ref[:, hd * LANES:(hd + 1) * LANES] = rope
        else:
            q_ref[:, c0:c0 + QK_NOPE] = nope.astype(BF16)
            q_ref[:, c0 + QK_NOPE:c0 + HEAD_PAD] = rope.astype(BF16)
            k_ref[:, c0:c0 + QK_NOPE] = kn[:, hd * QK_NOPE:(hd + 1) * QK_NOPE].astype(BF16)
            k_ref[:, c0 + QK_NOPE:c0 + HEAD_PAD] = krb


def _inproj_ab(x, g1, win, qg, kvg, wq, wqr, cos, sin, extra_w, *, heads, sample, tm, table_period):
    n, d = x.shape
    ql, kvl = qg.shape[1], kvg.shape[1]
    pd = win.shape[1] - ql - kvl - 2 * LANES
    nt = table_period // tm
    row = lambda i: (i, 0)
    const = lambda i: (0, 0)
    in_specs = [
        pl.BlockSpec((tm, d), row),
        _resident((1, d), const),
        _resident(win.shape, const),
        _resident((1, ql), const),
        _resident((1, kvl), const),
        _resident(wq.shape, const),
        _resident(wqr.shape, const),
        pl.BlockSpec((tm, LANES), lambda i: (i % nt, 0)),
        pl.BlockSpec((tm, LANES), lambda i: (i % nt, 0)),
    ] + [_resident(w.shape, (lambda i: (0, 0, 0)) if w.ndim == 3 else const) for w in extra_w]
    out_shape = [jax.ShapeDtypeStruct((n, pd), F32), jax.ShapeDtypeStruct((n, kvl), F32),
                 jax.ShapeDtypeStruct((n, QK_ROPE), F32)]
    out_specs = [pl.BlockSpec((tm, pd), row), pl.BlockSpec((tm, kvl), row), pl.BlockSpec((tm, QK_ROPE), row)]
    if sample:
        out_shape += [jax.ShapeDtypeStruct((n, heads * kvl), BF16), jax.ShapeDtypeStruct((n, heads * LANES), F32)]
        out_specs += [pl.BlockSpec((tm, heads * kvl), row), pl.BlockSpec((tm, heads * LANES), row)]
    else:
        out_shape += [jax.ShapeDtypeStruct((n, heads * HEAD_PAD), BF16), jax.ShapeDtypeStruct((n, heads * HEAD_PAD), BF16),
                      jax.ShapeDtypeStruct((n, heads * V_HEAD), BF16)]
        out_specs += [pl.BlockSpec((tm, heads * HEAD_PAD), row), pl.BlockSpec((tm, heads * HEAD_PAD), row),
                      pl.BlockSpec((tm, heads * V_HEAD), row)]
    return pl.pallas_call(
        functools.partial(_inproj_ab_kernel, heads=heads, sample=sample),
        grid=(n // tm,), in_specs=in_specs, out_specs=out_specs, out_shape=out_shape,
        compiler_params=_params(("arbitrary",), 52),
        name="inproj_ab_sample" if sample else "inproj_ab_prompt",
    )(x, g1, win, qg, kvg, wq, wqr, cos, sin, *extra_w)


def _pool_mix(pooled, pw_ref, ps_ref):
    outs = []
    for g in range(len(POOL_WINDOWS)):
        sl = slice(g * LANES, (g + 1) * LANES)
        outs.append(_dot(pooled[:, sl].astype(BF16), pw_ref[g]))
    return jnp.concatenate(outs, axis=-1) * ps_ref[...]


def _pool_prompt_kernel(u_ref, pw_ref, ps_ref, o_ref, ext_ref, *, seq, tc):
    halo = 16
    ext_ref[0:halo, :] = jnp.zeros((halo, ext_ref.shape[1]), F32)
    ext_ref[halo:halo + seq, :] = u_ref[...]
    for r0 in range(0, seq, tc):
        pos = r0 + lax.broadcasted_iota(jnp.int32, (tc, 1), 0)
        cols = []
        for g, w in enumerate(POOL_WINDOWS):
            sl = slice(g * LANES, (g + 1) * LANES)
            cur = ext_ref[halo + r0:halo + r0 + tc, sl]
            s = cur
            for j in range(1, w):
                s = s + ext_ref[halo + r0 - j:halo + r0 - j + tc, sl]
            cnt = jnp.minimum(pos + 1, w).astype(F32)
            cols.append(s / cnt - cur)
        pooled = jnp.concatenate(cols, axis=-1)
        o_ref[r0:r0 + tc, :] = _pool_mix(pooled, pw_ref, ps_ref).astype(o_ref.dtype)


def _pool_prompt(u, pw, ps, *, seq):
    n, pd = u.shape
    return pl.pallas_call(
        functools.partial(_pool_prompt_kernel, seq=seq, tc=256),
        grid=(n // seq,),
        in_specs=[pl.BlockSpec((seq, pd), lambda b: (b, 0)),
                  _resident(pw.shape, lambda b: (0, 0, 0)),
                  _resident((1, pd), lambda b: (0, 0))],
        out_specs=pl.BlockSpec((seq, pd), lambda b: (b, 0)),
        out_shape=jax.ShapeDtypeStruct((n, pd), BF16),
        scratch_shapes=[pltpu.VMEM((seq + 16, pd), F32)],
        compiler_params=_params(("arbitrary",), 32),
        name="pool_prompt",
    )(u, pw, ps)


def _pool_sample_kernel(st_ref, u_ref, pw_ref, ps_ref, o_ref, *, dec_seq, pos0):
    nb = st_ref.shape[0]

    def row(r):
        return (lambda sl: st_ref[r, :, sl]) if r < nb else (lambda sl: u_ref[r - nb, :, sl])

    for t in range(dec_seq):
        cols = []
        for g, w in enumerate(POOL_WINDOWS):
            sl = slice(g * LANES, (g + 1) * LANES)
            cur = row(nb + t)(sl)
            s = cur
            for j in range(1, w):
                s = s + row(nb + t - j)(sl)
            cnt = float(min(pos0 + t + 1, w))
            cols.append(s / cnt - cur)
        pooled = jnp.concatenate(cols, axis=-1)
        o_ref[t] = _pool_mix(pooled, pw_ref, ps_ref).astype(o_ref.dtype)


def _pool_sample(st, u, pw, ps, *, pos0):
    t, b, pd = u.shape
    full3 = lambda i: (0, 0, 0)
    return pl.pallas_call(
        functools.partial(_pool_sample_kernel, dec_seq=t, pos0=pos0),
        grid=(1,),
        in_specs=[pl.BlockSpec(st.shape, full3), pl.BlockSpec(u.shape, full3),
                  pl.BlockSpec(pw.shape, full3), pl.BlockSpec((1, pd), lambda i: (0, 0))],
        out_specs=pl.BlockSpec((t, b, pd), full3),
        out_shape=jax.ShapeDtypeStruct((t, b, pd), BF16),
        compiler_params=_params(("arbitrary",), 32),
        name="pool_sample",
    )(st, u, pw, ps)


def _softmax_step(s, m, l, acc, vv):
    m_new = jnp.maximum(m, jnp.max(s, axis=-1, keepdims=True))
    alpha = jnp.exp(m - m_new)
    p = jnp.exp(s - m_new)
    l = alpha * l + jnp.sum(p, axis=-1, keepdims=True)
    acc = alpha * acc + _dot(p.astype(BF16), vv)
    return m_new, l, acc


def _attn_prompt_kernel(q_ref, k_ref, v_ref, o_ref, *, tq, heads, heads_per_pass):
    qi = pl.program_id(1)
    causal = (lax.broadcasted_iota(jnp.int32, (tq, tq), 1) <= lax.broadcasted_iota(jnp.int32, (tq, tq), 0))
    for h0 in range(0, heads, heads_per_pass):
        hds = tuple(range(h0, h0 + heads_per_pass))
        qs = [q_ref[:, hd * HEAD_PAD:(hd + 1) * HEAD_PAD] for hd in hds]

        def step(k0, carry, masked, hds=hds, qs=qs):
            new = []
            for q, hd, (m, l, acc) in zip(qs, hds, carry):
                s = _dot_nt(q, k_ref[pl.ds(k0, tq), hd * HEAD_PAD:(hd + 1) * HEAD_PAD])
                if masked:
                    s = jnp.where(causal, s, -jnp.inf)
                new.append(_softmax_step(s, m, l, acc, v_ref[pl.ds(k0, tq), hd * V_HEAD:(hd + 1) * V_HEAD]))
            return tuple(new)

        init = tuple((jnp.full((tq, 1), -jnp.inf, F32), jnp.zeros((tq, 1), F32), jnp.zeros((tq, V_HEAD), F32))
                     for _ in hds)
        carry = lax.fori_loop(0, qi, lambda j, c: step(pl.multiple_of(j * tq, tq), c, False), init)
        carry = step(pl.multiple_of(qi * tq, tq), carry, True)
        for hd, (_, l, acc) in zip(hds, carry):
            o_ref[:, hd * V_HEAD:(hd + 1) * V_HEAD] = (acc / l).astype(o_ref.dtype)


def _attn_prompt(q, k, v, *, seq, heads, tq=512, heads_per_pass=2):
    n = q.shape[0]
    tq = min(tq, seq)
    nq = seq // tq
    return pl.pallas_call(
        functools.partial(_attn_prompt_kernel, tq=tq, heads=heads, heads_per_pass=heads_per_pass),
        grid=(n // seq, nq),
        in_specs=[pl.BlockSpec((tq, heads * HEAD_PAD), lambda b, i: (b * nq + i, 0)),
                  _resident((seq, heads * HEAD_PAD), lambda b, i: (b, 0)),
                  _resident((seq, heads * V_HEAD), lambda b, i: (b, 0))],
        out_specs=pl.BlockSpec((tq, heads * V_HEAD), lambda b, i: (b * nq + i, 0)),
        out_shape=jax.ShapeDtypeStruct((n, heads * V_HEAD), BF16),
        compiler_params=_params(("arbitrary", "arbitrary"), 48),
        name="attn_prompt",
    )(q, k, v)


def _attn_sample_kernel(pt_ref, ql_ref, qr_ref, latn_ref, krn_ref, clat_hbm, ckrt_hbm, o_ref,
                        latbuf, krbuf, newlat, newkr, sem, *, nseq, pages_per_chunk, n_chunks, page, dec_seq, heads,
                        n_slots, n_streams):
    s = pl.program_id(0)
    total = nseq * n_chunks
    ppc = pages_per_chunk

    def chunk_copies(g, slot):
        seq = lax.div(g, n_chunks)
        c = lax.rem(g, n_chunks)
        cps = []
        for p in range(ppc):
            pg = pt_ref[seq, c * ppc + p]
            cps.append(pltpu.make_async_copy(clat_hbm.at[pg], latbuf.at[slot, pl.ds(p * page, page)], sem.at[slot, 0]))
            cps.append(pltpu.make_async_copy(ckrt_hbm.at[pg], krbuf.at[slot, :, pl.ds(p * page, page)], sem.at[slot, 1]))
        return cps

    @pl.when(s == 0)
    def _():
        for g0 in range(n_slots):
            for cp in chunk_copies(jnp.int32(g0), g0):
                cp.start()

    ql = ql_ref[0]
    qr = qr_ref[0].astype(BF16)

    newlat[...] = jnp.zeros(newlat.shape, F32)
    newkr[...] = jnp.zeros(newkr.shape, F32)
    newlat[0:dec_seq, :] = latn_ref[0]
    newkr[0:dec_seq, :] = krn_ref[0]
    nl = newlat[...].astype(BF16)
    sc = _dot_nt(ql, nl) + _dot_nt(qr, newkr[...].astype(BF16))
    r_i = lax.broadcasted_iota(jnp.int32, sc.shape, 0)
    j_i = lax.broadcasted_iota(jnp.int32, sc.shape, 1)
    sc = jnp.where(j_i * heads <= r_i, sc, -jnp.inf)
    rows = sc.shape[0]
    empty = (jnp.full((rows, 1), -jnp.inf, F32), jnp.zeros((rows, 1), F32), jnp.zeros((rows, nl.shape[1]), F32))
    init = (_softmax_step(sc, *empty, nl),) + (empty,) * (n_streams - 1)
    part = ppc * page // n_streams

    def body(c, carry):
        g = s * n_chunks + c
        slot = lax.rem(g, n_slots)
        for cp in chunk_copies(g, slot):
            cp.wait()
        new = []
        for st, (m, l, acc) in enumerate(carry):
            lat = latbuf[slot, pl.ds(st * part, part), :].astype(BF16)
            krt = krbuf[slot, :, pl.ds(st * part, part)].astype(BF16)
            new.append(_softmax_step(_dot_nt(ql, lat) + _dot(qr, krt), m, l, acc, lat))
        for cp in chunk_copies(g + n_slots, slot):
            cp.start()
        return tuple(new)

    carry = lax.fori_loop(0, n_chunks, body, init)

    @pl.when(s == nseq - 1)
    def _():
        for i in range(n_slots):
            for cp in chunk_copies(jnp.int32(total + i), (total + i) % n_slots):
                cp.wait()

    m = functools.reduce(jnp.maximum, [st[0] for st in carry])
    l = sum(st[1] * jnp.exp(st[0] - m) for st in carry)
    acc = sum(st[2] * jnp.exp(st[0] - m) for st in carry)
    o_ref[0] = (acc / l).astype(o_ref.dtype)


def _attn_sample(page_table, ql, qr, latn, krn, cache_lat, cache_krt, *, heads, n_slots=3, n_streams=2):
    nseq, n_pages = page_table.shape
    _, page, kvl = cache_lat.shape
    rope = cache_krt.shape[1]
    rows = ql.shape[1]
    dec_seq = latn.shape[1]
    pages_per_chunk = min(16, n_pages)
    assert n_pages % pages_per_chunk == 0
    n_chunks = n_pages // pages_per_chunk
    spare_rows = -(-n_slots // n_chunks)
    assert spare_rows <= nseq and nseq * n_chunks >= n_slots
    page_table = jnp.concatenate([page_table, page_table[:spare_rows]], axis=0)
    chunk = pages_per_chunk * page
    per_seq = lambda s, pt: (s, 0, 0)
    grid_spec = pltpu.PrefetchScalarGridSpec(
        num_scalar_prefetch=1,
        grid=(nseq,),
        in_specs=[pl.BlockSpec((1, rows, kvl), per_seq),
                  pl.BlockSpec((1, rows, rope), per_seq),
                  pl.BlockSpec((1, dec_seq, kvl), per_seq),
                  pl.BlockSpec((1, dec_seq, rope), per_seq),
                  pl.BlockSpec(memory_space=pl.ANY),
                  pl.BlockSpec(memory_space=pl.ANY)],
        out_specs=pl.BlockSpec((1, rows, kvl), per_seq),
        scratch_shapes=[pltpu.VMEM((n_slots, chunk, kvl), F32), pltpu.VMEM((n_slots, rope, chunk), F32),
                        pltpu.VMEM((LANES, kvl), F32), pltpu.VMEM((LANES, rope), F32),
                        pltpu.SemaphoreType.DMA((n_slots, 2))],
    )
    return pl.pallas_call(
        functools.partial(_attn_sample_kernel, nseq=nseq, pages_per_chunk=pages_per_chunk, n_chunks=n_chunks,
                          page=page, dec_seq=dec_seq, heads=heads, n_slots=n_slots, n_streams=n_streams),
        grid_spec=grid_spec,
        out_shape=jax.ShapeDtypeStruct((nseq, rows, kvl), BF16),
        compiler_params=_params(("arbitrary",), 44),
        name="attn_sample",
    )(page_table, ql, qr, latn, krn, cache_lat, cache_krt)


def _unabsorb_kernel(o_ref, wuv_ref, b_ref, *, heads):
    kvl = wuv_ref.shape[1]
    for hd in range(heads):
        b_ref[:, hd * V_HEAD:(hd + 1) * V_HEAD] = _dot(o_ref[:, hd * kvl:(hd + 1) * kvl], wuv_ref[hd]).astype(b_ref.dtype)


def _unabsorb(o_lat, wuv_h, *, heads, tm):
    n = o_lat.shape[0]
    return pl.pallas_call(
        functools.partial(_unabsorb_kernel, heads=heads),
        grid=(n // tm,),
        in_specs=[pl.BlockSpec((tm, o_lat.shape[1]), lambda i: (i, 0)),
                  _resident(wuv_h.shape, lambda i: (0, 0, 0))],
        out_specs=pl.BlockSpec((tm, heads * V_HEAD), lambda i: (i, 0)),
        out_shape=jax.ShapeDtypeStruct((n, heads * V_HEAD), BF16),
        compiler_params=_params(("arbitrary",), 32),
        name="unabsorb",
    )(o_lat, wuv_h)


def _outproj_kernel(*refs, widths):
    x_ref = refs[0]
    parts = refs[1:1 + len(widths)]
    w_ref = refs[1 + len(widths)]
    o_ref = refs[2 + len(widths)]
    acc = x_ref[...]
    off = 0
    for pr, w in zip(parts, widths):
        acc = acc + _dot(pr[...], w_ref[off:off + w, :])
        off += w
    o_ref[...] = acc


def _outproj(x, parts, w, *, tm):
    n, d = x.shape
    widths = tuple(p.shape[1] for p in parts)
    row = lambda i: (i, 0)
    return pl.pallas_call(
        functools.partial(_outproj_kernel, widths=widths),
        grid=(n // tm,),
        in_specs=[pl.BlockSpec((tm, d), row)] + [pl.BlockSpec((tm, wd), row) for wd in widths]
                 + [_resident(w.shape, lambda i: (0, 0))],
        out_specs=pl.BlockSpec((tm, d), row),
        out_shape=jax.ShapeDtypeStruct((n, d), F32),
        compiler_params=_params(("arbitrary",), 40),
        name="outproj",
    )(x, *parts, w)


def _mlp_kernel(x_ref, g_ref, wup_ref, wdn_ref, fg_ref, o_ref, xn_ref, *, final):
    j = pl.program_id(1)

    @pl.when(j == 0)
    def _():
        x = x_ref[...]
        xn_ref[...] = _rms(x, g_ref[...]).astype(BF16)
        o_ref[...] = x

    a = jnp.maximum(_dot(xn_ref[...], wup_ref[...]), 0.0)
    o_ref[...] += _dot((a * a).astype(BF16), wdn_ref[...])

    if final:
        @pl.when(j == pl.num_programs(1) - 1)
        def _():
            o_ref[...] = _rms(o_ref[...], fg_ref[...])


def _mlp(x, g, wup, wdn, fg, *, layer, final, tm, tf):
    n, d = x.shape
    dff = wup.shape[2]
    return pl.pallas_call(
        functools.partial(_mlp_kernel, final=final),
        grid=(n // tm, dff // tf),
        in_specs=[pl.BlockSpec((tm, d), lambda i, j: (i, 0)),
                  _resident((1, d), lambda i, j: (0, 0)),
                  pl.BlockSpec((None, d, tf), lambda i, j: (layer, 0, j)),
                  pl.BlockSpec((None, tf, d), lambda i, j: (layer, j, 0)),
                  _resident((1, d), lambda i, j: (0, 0))],
        out_specs=pl.BlockSpec((tm, d), lambda i, j: (i, 0)),
        out_shape=jax.ShapeDtypeStruct((n, d), F32),
        scratch_shapes=[pltpu.VMEM((tm, d), BF16)],
        compiler_params=_params(("arbitrary", "arbitrary"), 48),
        name="mlp_final" if final else "mlp",
    )(x, g, wup, wdn, fg)


def _inproj_cd_kernel(x_ref, g_ref, wv_ref, wg_ref, wb_ref, wc_ref, wh_ref, glu_ref, cgh_ref, bg_ref, xn_ref):
    j = pl.program_id(1)

    @pl.when(j == 0)
    def _():
        xn_ref[...] = _rms(x_ref[...], g_ref[...]).astype(BF16)

    xn = xn_ref[...]
    glu_ref[...] = _dot(xn, wv_ref[...]) * jax.nn.sigmoid(_dot(xn, wg_ref[...]))
    bg_ref[...] = _dot(xn, wb_ref[...])
    cgh_ref[...] = _dot(xn, wc_ref[...]) * _dot(xn, wh_ref[...])


def _inproj_cd(x, g, w, *, tm, tw):
    n, d = x.shape
    cdim = w.shape[1] // 5
    nct = cdim // tw
    out = jax.ShapeDtypeStruct((n, cdim), F32)
    ospec = pl.BlockSpec((tm, tw), lambda i, j: (i, j))
    return pl.pallas_call(
        _inproj_cd_kernel,
        grid=(n // tm, nct),
        in_specs=[pl.BlockSpec((tm, d), lambda i, j: (i, 0)),
                  _resident((1, d), lambda i, j: (0, 0))]
                 + [pl.BlockSpec((d, tw), lambda i, j, k=k: (0, k * nct + j)) for k in range(5)],
        out_specs=[ospec, ospec, ospec],
        out_shape=[out, out, out],
        scratch_shapes=[pltpu.VMEM((tm, d), BF16)],
        compiler_params=_params(("arbitrary", "arbitrary"), 48),
        name="inproj_cd",
    )(x, g, w, w, w, w, w)


def _ln_silu(y, g, b):
    mu = jnp.mean(y, axis=-1, keepdims=True)
    yc = y - mu
    var = jnp.mean(yc * yc, axis=-1, keepdims=True)
    z = yc * lax.rsqrt(var + EPS) * g + b
    return z * jax.nn.sigmoid(z)


def _conv_prompt_kernel(glu_ref, cgh_ref, bg_ref, cw_ref, cb_ref, lg_ref, lb_ref, sw_ref, o_ref,
                        ext_ref, sh_ref, ext2_ref, y_ref, *, tt, rc, lt):
    tb = pl.program_id(1)
    kw = cw_ref.shape[0]
    sk = sw_ref.shape[0]
    cdim = glu_ref.shape[1]
    sub = 8
    halo, halo2 = 32, 8

    @pl.when(tb == 0)
    def _():
        ext_ref[0:halo, :] = jnp.zeros((halo, cdim), F32)
        ext2_ref[0:halo2, :] = jnp.zeros((halo2, cdim), F32)

    ext_ref[halo:halo + tt, :] = glu_ref[...]
    ext2_ref[halo2:halo2 + tt, :] = cgh_ref[...]
    span = halo + tt - sub
    for j in range(1, sub):
        sh_ref[j - 1, 0:span, :] = ext_ref[j:j + span, :]

    for l0 in range(0, cdim, lt):
        ls = slice(l0, l0 + lt)
        for r0 in range(0, tt, rc):
            acc = jnp.zeros((rc, lt), F32)
            for k in range(kw):
                off = halo - (kw - 1) + k
                j, base = off % sub, r0 + off - off % sub
                rows = ext_ref[base:base + rc, ls] if j == 0 else sh_ref[j - 1, base:base + rc, ls]
                acc = acc + rows * cw_ref[k:k + 1, ls]
            y_ref[r0:r0 + rc, ls] = acc + cb_ref[:, ls]
            sc = jnp.zeros((rc, lt), F32)
            for k in range(sk):
                base = halo2 - (sk - 1) + r0 + k
                sc = sc + ext2_ref[base:base + rc, ls] * sw_ref[k:k + 1, ls]
            o_ref[r0:r0 + rc, cdim + l0:cdim + l0 + lt] = (bg_ref[r0:r0 + rc, ls] * sc).astype(o_ref.dtype)

    o_ref[:, 0:cdim] = _ln_silu(y_ref[...], lg_ref[...], lb_ref[...]).astype(o_ref.dtype)

    ext_ref[0:halo, :] = ext_ref[tt:tt + halo, :]
    ext2_ref[0:halo2, :] = ext2_ref[tt:tt + halo2, :]


def _conv_prompt(glu, cgh, bg, cw, cb, lg, lb, sw, *, seq, tt=256):
    n, cdim = glu.shape
    nt = seq // tt
    blk = lambda b, t: (b * nt + t, 0)
    c2 = lambda b, t: (0, 0)
    return pl.pallas_call(
        functools.partial(_conv_prompt_kernel, tt=tt, rc=tt // 2, lt=LANES),
        grid=(n // seq, nt),
        in_specs=[pl.BlockSpec((tt, cdim), blk), pl.BlockSpec((tt, cdim), blk), pl.BlockSpec((tt, cdim), blk),
                  _resident(cw.shape, c2), _resident((1, cdim), c2), _resident((1, cdim), c2),
                  _resident((1, cdim), c2), _resident(sw.shape, c2)],
        out_specs=pl.BlockSpec((tt, 2 * cdim), blk),
        out_shape=jax.ShapeDtypeStruct((n, 2 * cdim), BF16),
        scratch_shapes=[pltpu.VMEM((tt + 32, cdim), F32), pltpu.VMEM((7, tt + 24, cdim), F32),
                        pltpu.VMEM((tt + 8, cdim), F32), pltpu.VMEM((tt, cdim), F32)],
        compiler_params=_params(("arbitrary", "arbitrary"), 40),
        name="conv_prompt",
    )(glu, cgh, bg, cw, cb, lg, lb, sw)


def _conv_sample_kernel(cst_ref, sst_ref, glu_ref, cgh_ref, bg_ref, cw_ref, cb_ref, lg_ref, lb_ref, sw_ref, o_ref,
                        *, dec_seq):
    kw = cw_ref.shape[0]
    sk = sw_ref.shape[0]
    cdim = glu_ref.shape[2]

    def ext(r):
        return cst_ref[r] if r < kw - 1 else glu_ref[r - (kw - 1)]

    def ext2(r):
        return sst_ref[r] if r < sk - 1 else cgh_ref[r - (sk - 1)]

    for t in range(dec_seq):
        acc = ext(t) * cw_ref[0:1, :]
        for k in range(1, kw):
            acc = acc + ext(t + k) * cw_ref[k:k + 1, :]
        o_ref[t, :, 0:cdim] = _ln_silu(acc + cb_ref[...], lg_ref[...], lb_ref[...]).astype(o_ref.dtype)
        sc = ext2(t) * sw_ref[0:1, :]
        for k in range(1, sk):
            sc = sc + ext2(t + k) * sw_ref[k:k + 1, :]
        o_ref[t, :, cdim:2 * cdim] = (bg_ref[t] * sc).astype(o_ref.dtype)


def _conv_sample(cst, sst, glu, cgh, bg, cw, cb, lg, lb, sw, *, tb=32):
    t, b, cdim = glu.shape
    blk = lambda i: (0, i, 0)
    c2 = lambda i: (0, 0)
    new = pl.BlockSpec((t, tb, cdim), blk)
    return pl.pallas_call(
        functools.partial(_conv_sample_kernel, dec_seq=t),
        grid=(b // tb,),
        in_specs=[pl.BlockSpec((cst.shape[0], tb, cdim), blk), pl.BlockSpec((sst.shape[0], tb, cdim), blk),
                  new, new, new,
                  _resident(cw.shape, c2), _resident((1, cdim), c2), _resident((1, cdim), c2),
                  _resident((1, cdim), c2), _resident(sw.shape, c2)],
        out_specs=pl.BlockSpec((t, tb, 2 * cdim), blk),
        out_shape=jax.ShapeDtypeStruct((t, b, 2 * cdim), BF16),
        compiler_params=_params(("arbitrary",), 32),
        name="conv_sample",
    )(cst, sst, glu, cgh, bg, cw, cb, lg, lb, sw)


def _rope_tables(pos):
    half = QK_ROPE // 2
    inv = ROPE_THETA ** (-jnp.arange(half, dtype=F32) / half)
    ang = pos.astype(F32)[:, None] * inv[None, :]
    z = jnp.zeros((pos.shape[0], LANES - QK_ROPE), F32)
    cos, sin = jnp.cos(ang), jnp.sin(ang)
    return jnp.concatenate([cos, cos, z], axis=-1), jnp.concatenate([sin, sin, z], axis=-1)


def _rot_cols(w):
    half = QK_ROPE // 2
    return jnp.concatenate([-w[..., half:], w[..., :half]], axis=-1)


def _pad_cols(w, width):
    return jnp.concatenate([w, jnp.zeros(w.shape[:-1] + (width - w.shape[-1],), w.dtype)], axis=-1)


def _tail(buf, new, keep):
    new_tail = new[:, max(0, new.shape[1] - keep):]
    return jnp.concatenate([buf.astype(new.dtype), new_tail], axis=1)[:, -keep:]


def kernel(x_prompt, x_sample, cache_mla_latent, cache_mla_krope, page_table, state_pool, state_ccv, state_sconv,
           norm1_g, norm2_g, final_norm_g, w_in_ab, pool_w, pool_scale, q_norm_g, w_uq, kv_norm_g, w_uk, w_uv,
           w_out_ab, w_in_cd, ccv_w, ccv_b, ccv_ln_g, ccv_ln_b, sconv_w, w_out_cd, w_up, w_down):
    bsz, seq, d = x_prompt.shape
    dbsz, dseq, _ = x_sample.shape
    depth = norm1_g.shape[0]
    page = cache_mla_latent.shape[2]
    past_len = page_table.shape[1] * page
    heads = w_uq.shape[2]
    pd = state_pool.shape[-1]
    ql = q_norm_g.shape[-1]
    kvl = kv_norm_g.shape[-1]
    cdim = state_ccv.shape[-1]
    n_p, n_s = bsz * seq, dbsz * dseq
    tm_p = 256 if n_p % 256 == 0 else n_p
    tm_s = 256 if n_s % 256 == 0 else n_s
    tmm_p = 512 if n_p % 512 == 0 else tm_p
    tmm_s = 512 if n_s % 512 == 0 else tm_s

    hp = x_prompt.reshape(n_p, d)
    hs = x_sample.transpose(1, 0, 2).reshape(n_s, d)

    cos_p, sin_p = _rope_tables(jnp.arange(seq, dtype=jnp.int32))
    pos_s = jnp.repeat(past_len + jnp.arange(dseq, dtype=jnp.int32), dbsz)
    cos_s, sin_s = _rope_tables(pos_s)

    def row2(v):
        return v.reshape(1, -1)

    def to_bt(a):
        return a.reshape(dseq, dbsz, -1).transpose(1, 0, 2)

    wup = w_up.astype(BF16)
    wdn = w_down.astype(BF16)
    fg = row2(final_norm_g)
    p_lat, p_kr, s_lat, s_kr, p_pool, s_pool = [], [], [], [], [], []
    p_ccv, s_ccv, p_sc, s_sc = [], [], [], []
    for layer in range(depth):
        i = layer // 2
        last = layer == depth - 1
        if layer % 2 == 0:
            wi = w_in_ab[i]
            o3 = pd + ql + kvl
            k_r = wi[:, o3:]
            win = jnp.concatenate([wi[:, :o3], _pad_cols(k_r, LANES), _pad_cols(_rot_cols(k_r), LANES)], axis=-1).astype(BF16)
            wq_h = w_uq[i]
            wq = _pad_cols(wq_h, HEAD_PAD).reshape(ql, heads * HEAD_PAD).astype(BF16)
            wqr = _pad_cols(_rot_cols(wq_h[..., QK_NOPE:]), LANES).reshape(ql, heads * LANES).astype(BF16)
            wuk2 = w_uk[i].reshape(kvl, heads * QK_NOPE).astype(BF16)
            wuv2 = w_uv[i].reshape(kvl, heads * V_HEAD).astype(BF16)
            wuk_t = w_uk[i].transpose(1, 2, 0).astype(BF16)
            wuv_h = w_uv[i].transpose(1, 0, 2).astype(BF16)
            pw = pool_w[i].astype(BF16)
            ps = row2(pool_scale[i])
            wo = w_out_ab[i].astype(BF16)
            common = (row2(norm1_g[layer]), win, row2(q_norm_g[i]), row2(kv_norm_g[i]), wq, wqr)

            u, lat, kr, q, k, v = _inproj_ab(hp, *common, cos_p, sin_p, (wuk2, wuv2), heads=heads, sample=False,
                                             tm=tm_p, table_period=seq)
            a_out = _pool_prompt(u, pw, ps, seq=seq)
            b_out = _attn_prompt(q, k, v, seq=seq, heads=heads)
            hp = _outproj(hp, [a_out, b_out], wo, tm=tmm_p)
            p_lat.append(lat.reshape(bsz, seq, kvl))
            p_kr.append(kr.reshape(bsz, seq, QK_ROPE))
            p_pool.append(_tail(jnp.zeros((bsz, POOL_BUF, pd), F32), u.reshape(bsz, seq, pd), POOL_BUF))

            u, lat, kr, qlat, qrope = _inproj_ab(hs, *common, cos_s, sin_s, (wuk_t,), heads=heads, sample=True,
                                                 tm=tm_s, table_period=n_s)
            a_out = _pool_sample(state_pool[i].transpose(1, 0, 2), u.reshape(dseq, dbsz, pd), pw, ps, pos0=past_len)
            ql_b = qlat.reshape(dseq, dbsz, heads, kvl).transpose(1, 0, 2, 3).reshape(dbsz, dseq * heads, kvl)
            qr_b = (qrope.reshape(dseq, dbsz, heads, LANES)[..., :QK_ROPE]
                    .transpose(1, 0, 2, 3).reshape(dbsz, dseq * heads, QK_ROPE))
            lat_b, kr_b = to_bt(lat), to_bt(kr)
            o_lat = _attn_sample(page_table, ql_b, qr_b, lat_b, kr_b, cache_mla_latent[i],
                                 jnp.swapaxes(cache_mla_krope[i], 1, 2), heads=heads)
            o_tm = o_lat.reshape(dbsz, dseq, heads * kvl).transpose(1, 0, 2).reshape(n_s, heads * kvl)
            b_out = _unabsorb(o_tm, wuv_h, heads=heads, tm=tm_s)
            hs = _outproj(hs, [a_out.reshape(n_s, pd), b_out], wo, tm=tmm_s)
            s_lat.append(lat_b)
            s_kr.append(kr_b)
            s_pool.append(_tail(state_pool[i], to_bt(u), POOL_BUF))
        else:
            tw = 512
            wcd = w_in_cd[i].astype(BF16)
            conv_w = (ccv_w[i], row2(ccv_b[i]), row2(ccv_ln_g[i]), row2(ccv_ln_b[i]), sconv_w[i])
            wo = w_out_cd[i].astype(BF16)
            kw, sk = ccv_w.shape[1], sconv_w.shape[1]

            glu, cgh, bg = _inproj_cd(hp, row2(norm1_g[layer]), wcd, tm=tmm_p, tw=tw)
            cd = _conv_prompt(glu, cgh, bg, *conv_w, seq=seq)
            hp = _outproj(hp, [cd], wo, tm=tmm_p)
            p_ccv.append(_tail(jnp.zeros((bsz, kw - 1, cdim), F32), glu.reshape(bsz, seq, cdim), kw - 1))
            p_sc.append(_tail(jnp.zeros((bsz, sk - 1, cdim), F32), cgh.reshape(bsz, seq, cdim), sk - 1))

            glu, cgh, bg = _inproj_cd(hs, row2(norm1_g[layer]), wcd, tm=tmm_s, tw=tw)
            cd = _conv_sample(state_ccv[i].transpose(1, 0, 2), state_sconv[i].transpose(1, 0, 2),
                              glu.reshape(dseq, dbsz, cdim), cgh.reshape(dseq, dbsz, cdim),
                              bg.reshape(dseq, dbsz, cdim), *conv_w)
            hs = _outproj(hs, [cd.reshape(n_s, 2 * cdim)], wo, tm=tmm_s)
            s_ccv.append(_tail(state_ccv[i], to_bt(glu), kw - 1))
            s_sc.append(_tail(state_sconv[i], to_bt(cgh), sk - 1))

        hp = _mlp(hp, row2(norm2_g[layer]), wup, wdn, fg, layer=layer, final=last, tm=tmm_p, tf=1024)
        hs = _mlp(hs, row2(norm2_g[layer]), wup, wdn, fg, layer=layer, final=last, tm=tmm_s, tf=1024)

    y_prompt = hp.reshape(bsz, seq, d)
    y_sample = to_bt(hs)
    return (y_prompt, y_sample,
            jnp.stack(p_lat), jnp.stack(p_kr), jnp.stack(s_lat), jnp.stack(s_kr),
            jnp.stack(p_pool), jnp.stack(s_pool),
            jnp.stack(p_ccv), jnp.stack(s_ccv),
            jnp.stack(p_sc), jnp.stack(s_sc))
```

```python
import functools

import jax
import jax.numpy as jnp
from jax import lax
from jax.experimental import pallas as pl
from jax.experimental.pallas import tpu as pltpu

F32 = jnp.float32
BF16 = jnp.bfloat16

EPS = 1e-6
POOL_WINDOWS = (2, 4, 8, 16)
POOL_BUF = max(POOL_WINDOWS) - 1
LANES = 128
HEAD_PAD = 256
QK_NOPE = 128
QK_ROPE = 64
V_HEAD = 128
ROPE_THETA = 10000.0
MLA_SCALE = (QK_NOPE + QK_ROPE) ** -0.5
MIB = 1024 * 1024


def _params(semantics, vmem_mib):
    return pltpu.CompilerParams(dimension_semantics=semantics, vmem_limit_bytes=vmem_mib * MIB)


def _resident(shape, index_map):
    return pl.BlockSpec(shape, index_map, pipeline_mode=pl.Buffered(1))


def _rms(x, g):
    ms = jnp.mean(x * x, axis=-1, keepdims=True)
    return x * lax.rsqrt(ms + EPS) * g


def _dot(a, b):
    return jnp.dot(a, b, preferred_element_type=F32)


def _dot_nt(a, b):
    return lax.dot_general(a, b, (((1,), (1,)), ((), ())), preferred_element_type=F32)


def _inproj_ab_kernel(x_ref, g1_ref, win_ref, qg_ref, kvg_ref, wq_ref, wqr_ref, cos_ref, sin_ref, *rest,
                      heads, sample):
    if sample:
        wukt_ref, u_ref, lat_ref, kr_ref, qlat_ref, qrope_ref = rest
    else:
        wuk_ref, wuv_ref, u_ref, lat_ref, kr_ref, q_ref, k_ref, v_ref = rest
    pd = u_ref.shape[1]
    ql = qg_ref.shape[1]
    kvl = kvg_ref.shape[1]
    h = _rms(x_ref[...], g1_ref[...]).astype(BF16)
    z = _dot(h, win_ref[...])
    o1, o2, o3 = pd, pd + ql, pd + ql + kvl
    u_ref[...] = z[:, :o1]
    lat = _rms(z[:, o2:o3], kvg_ref[...])
    lat_ref[...] = lat
    cos = cos_ref[...]
    sin = sin_ref[...]
    krope = z[:, o3:o3 + LANES] * cos + z[:, o3 + LANES:o3 + 2 * LANES] * sin
    kr_ref[...] = krope[:, :QK_ROPE]
    cqn = _rms(z[:, o1:o2], qg_ref[...]).astype(BF16)
    qm = _dot(cqn, wq_ref[...])
    qrot = _dot(cqn, wqr_ref[...])
    if not sample:
        latb = lat.astype(BF16)
        kn = _dot(latb, wuk_ref[...])
        v_ref[...] = _dot(latb, wuv_ref[...]).astype(BF16)
        krb = krope.astype(BF16)
    for hd in range(heads):
        c0 = hd * HEAD_PAD
        nope = qm[:, c0:c0 + QK_NOPE] * MLA_SCALE
        rope = (qm[:, c0 + QK_NOPE:c0 + HEAD_PAD] * cos + qrot[:, hd * LANES:(hd + 1) * LANES] * sin) * MLA_SCALE
        if sample:
            qlat_ref[:, hd * kvl:(hd + 1) * kvl] = _dot(nope.astype(BF16), wukt_ref[hd]).astype(BF16)
            qrope_ref[:, hd * LANES:(hd + 1) * LANES] = rope
        else:
            q_ref[:, c0:c0 + QK_NOPE] = nope.astype(BF16)
            q_ref[:, c0 + QK_NOPE:c0 + HEAD_PAD] = rope.astype(BF16)
            k_ref[:, c0:c0 + QK_NOPE] = kn[:, hd * QK_NOPE:(hd + 1) * QK_NOPE].astype(BF16)
            k_ref[:, c0 + QK_NOPE:c0 + HEAD_PAD] = krb


def _inproj_ab(x, g1, win, qg, kvg, wq, wqr, cos, sin, extra_w, *, heads, sample, tm, table_period):
    n, d = x.shape
    ql, kvl = qg.shape[1], kvg.shape[1]
    pd = win.shape[1] - ql - kvl - 2 * LANES
    nt = table_period // tm
    row = lambda i: (i, 0)
    const = lambda i: (0, 0)
    in_specs = [
        pl.BlockSpec((tm, d), row),
        _resident((1, d), const),
        _resident(win.shape, const),
        _resident((1, ql), const),
        _resident((1, kvl), const),
        _resident(wq.shape, const),
        _resident(wqr.shape, const),
        pl.BlockSpec((tm, LANES), lambda i: (i % nt, 0)),
        pl.BlockSpec((tm, LANES), lambda i: (i % nt, 0)),
    ] + [_resident(w.shape, (lambda i: (0, 0, 0)) if w.ndim == 3 else const) for w in extra_w]
    out_shape = [jax.ShapeDtypeStruct((n, pd), F32), jax.ShapeDtypeStruct((n, kvl), F32),
                 jax.ShapeDtypeStruct((n, QK_ROPE), F32)]
    out_specs = [pl.BlockSpec((tm, pd), row), pl.BlockSpec((tm, kvl), row), pl.BlockSpec((tm, QK_ROPE), row)]
    if sample:
        out_shape += [jax.ShapeDtypeStruct((n, heads * kvl), BF16), jax.ShapeDtypeStruct((n, heads * LANES), F32)]
        out_specs += [pl.BlockSpec((tm, heads * kvl), row), pl.BlockSpec((tm, heads * LANES), row)]
    else:
        out_shape += [jax.ShapeDtypeStruct((n, heads * HEAD_PAD), BF16), jax.ShapeDtypeStruct((n, heads * HEAD_PAD), BF16),
                      jax.ShapeDtypeStruct((n, heads * V_HEAD), BF16)]
        out_specs += [pl.BlockSpec((tm, heads * HEAD_PAD), row), pl.BlockSpec((tm, heads * HEAD_PAD), row),
                      pl.BlockSpec((tm, heads * V_HEAD), row)]
    return pl.pallas_call(
        functools.partial(_inproj_ab_kernel, heads=heads, sample=sample),
        grid=(n // tm,), in_specs=in_specs, out_specs=out_specs, out_shape=out_shape,
        compiler_params=_params(("arbitrary",), 52),
        name="inproj_ab_sample" if sample else "inproj_ab_prompt",
    )(x, g1, win, qg, kvg, wq, wqr, cos, sin, *extra_w)


def _pool_mix(pooled, pw_ref, ps_ref):
    outs = []
    for g in range(len(POOL_WINDOWS)):
        sl = slice(g * LANES, (g + 1) * LANES)
        outs.append(_dot(pooled[:, sl].astype(BF16), pw_ref[g]))
    return jnp.concatenate(outs, axis=-1) * ps_ref[...]


def _pool_prompt_kernel(u_ref, pw_ref, ps_ref, o_ref, ext_ref, *, seq, tc):
    halo = 16
    ext_ref[0:halo, :] = jnp.zeros((halo, ext_ref.shape[1]), F32)
    ext_ref[halo:halo + seq, :] = u_ref[...]
    for r0 in range(0, seq, tc):
        pos = r0 + lax.broadcasted_iota(jnp.int32, (tc, 1), 0)
        cols = []
        for g, w in enumerate(POOL_WINDOWS):
            sl = slice(g * LANES, (g + 1) * LANES)
            cur = ext_ref[halo + r0:halo + r0 + tc, sl]
            s = cur
            for j in range(1, w):
                s = s + ext_ref[halo + r0 - j:halo + r0 - j + tc, sl]
            cnt = jnp.minimum(pos + 1, w).astype(F32)
            cols.append(s / cnt - cur)
        pooled = jnp.concatenate(cols, axis=-1)
        o_ref[r0:r0 + tc, :] = _pool_mix(pooled, pw_ref, ps_ref).astype(o_ref.dtype)


def _pool_prompt(u, pw, ps, *, seq):
    n, pd = u.shape
    return pl.pallas_call(
        functools.partial(_pool_prompt_kernel, seq=seq, tc=256),
        grid=(n // seq,),
        in_specs=[pl.BlockSpec((seq, pd), lambda b: (b, 0)),
                  _resident(pw.shape, lambda b: (0, 0, 0)),
                  _resident((1, pd), lambda b: (0, 0))],
        out_specs=pl.BlockSpec((seq, pd), lambda b: (b, 0)),
        out_shape=jax.ShapeDtypeStruct((n, pd), BF16),
        scratch_shapes=[pltpu.VMEM((seq + 16, pd), F32)],
        compiler_params=_params(("arbitrary",), 32),
        name="pool_prompt",
    )(u, pw, ps)


def _pool_sample_kernel(st_ref, u_ref, pw_ref, ps_ref, o_ref, *, dec_seq, pos0):
    nb = st_ref.shape[0]

    def row(r):
        return (lambda sl: st_ref[r, :, sl]) if r < nb else (lambda sl: u_ref[r - nb, :, sl])

    for t in range(dec_seq):
        cols = []
        for g, w in enumerate(POOL_WINDOWS):
            sl = slice(g * LANES, (g + 1) * LANES)
            cur = row(nb + t)(sl)
            s = cur
            for j in range(1, w):
                s = s + row(nb + t - j)(sl)
            cnt = float(min(pos0 + t + 1, w))
            cols.append(s / cnt - cur)
        pooled = jnp.concatenate(cols, axis=-1)
        o_ref[t] = _pool_mix(pooled, pw_ref, ps_ref).astype(o_ref.dtype)


def _pool_sample(st, u, pw, ps, *, pos0):
    t, b, pd = u.shape
    full3 = lambda i: (0, 0, 0)
    return pl.pallas_call(
        functools.partial(_pool_sample_kernel, dec_seq=t, pos0=pos0),
        grid=(1,),
        in_specs=[pl.BlockSpec(st.shape, full3), pl.BlockSpec(u.shape, full3),
                  pl.BlockSpec(pw.shape, full3), pl.BlockSpec((1, pd), lambda i: (0, 0))],
        out_specs=pl.BlockSpec((t, b, pd), full3),
        out_shape=jax.ShapeDtypeStruct((t, b, pd), BF16),
        compiler_params=_params(("arbitrary",), 32),
        name="pool_sample",
    )(st, u, pw, ps)


def _softmax_step(s, m, l, acc, vv):
    m_new = jnp.maximum(m, jnp.max(s, axis=-1, keepdims=True))
    alpha = jnp.exp(m - m_new)
    p = jnp.exp(s - m_new)
    l = alpha * l + jnp.sum(p, axis=-1, keepdims=True)
    acc = alpha * acc + _dot(p.astype(BF16), vv)
    return m_new, l, acc


def _attn_prompt_kernel(q_ref, k_ref, v_ref, o_ref, *, tq, heads, heads_per_pass):
    qi = pl.program_id(1)
    causal = (lax.broadcasted_iota(jnp.int32, (tq, tq), 1) <= lax.broadcasted_iota(jnp.int32, (tq, tq), 0))
    for h0 in range(0, heads, heads_per_pass):
        hds = tuple(range(h0, h0 + heads_per_pass))
        qs = [q_ref[:, hd * HEAD_PAD:(hd + 1) * HEAD_PAD] for hd in hds]

        def step(k0, carry, masked, hds=hds, qs=qs):
            scs = [_dot_nt(q, k_ref[pl.ds(k0, tq), hd * HEAD_PAD:(hd + 1) * HEAD_PAD]) for q, hd in zip(qs, hds)]
            if masked:
                scs = [jnp.where(causal, s, -jnp.inf) for s in scs]
            return tuple(_softmax_step(s, m, l, acc, v_ref[pl.ds(k0, tq), hd * V_HEAD:(hd + 1) * V_HEAD])
                         for s, hd, (m, l, acc) in zip(scs, hds, carry))

        init = tuple((jnp.full((tq, 1), -jnp.inf, F32), jnp.zeros((tq, 1), F32), jnp.zeros((tq, V_HEAD), F32))
                     for _ in hds)
        carry = lax.fori_loop(0, qi, lambda j, c: step(pl.multiple_of(j * tq, tq), c, False), init)
        carry = step(pl.multiple_of(qi * tq, tq), carry, True)
        for hd, (_, l, acc) in zip(hds, carry):
            o_ref[:, hd * V_HEAD:(hd + 1) * V_HEAD] = (acc / l).astype(o_ref.dtype)


def _attn_prompt(q, k, v, *, seq, heads, tq=512, heads_per_pass=2):
    n = q.shape[0]
    tq = min(tq, seq)
    nq = seq // tq
    return pl.pallas_call(
        functools.partial(_attn_prompt_kernel, tq=tq, heads=heads, heads_per_pass=heads_per_pass),
        grid=(n // seq, nq),
        in_specs=[pl.BlockSpec((tq, heads * HEAD_PAD), lambda b, i: (b * nq + i, 0)),
                  _resident((seq, heads * HEAD_PAD), lambda b, i: (b, 0)),
                  _resident((seq, heads * V_HEAD), lambda b, i: (b, 0))],
        out_specs=pl.BlockSpec((tq, heads * V_HEAD), lambda b, i: (b * nq + i, 0)),
        out_shape=jax.ShapeDtypeStruct((n, heads * V_HEAD), BF16),
        compiler_params=_params(("arbitrary", "arbitrary"), 48),
        name="attn_prompt",
    )(q, k, v)


def _attn_sample_kernel(pt_ref, ql_ref, qr_ref, latn_ref, krn_ref, clat_hbm, ckrt_hbm, o_ref,
                        latbuf, krbuf, newlat, newkr, sem, *, nseq, pages_per_chunk, n_chunks, page, dec_seq, heads,
                        n_slots, n_streams):
    s = pl.program_id(0)
    total = nseq * n_chunks
    ppc = pages_per_chunk

    def chunk_copies(g, slot):
        seq = lax.div(g, n_chunks)
        c = lax.rem(g, n_chunks)
        cps = []
        for p in range(ppc):
            pg = pt_ref[seq, c * ppc + p]
            cps.append(pltpu.make_async_copy(clat_hbm.at[pg], latbuf.at[slot, pl.ds(p * page, page)], sem.at[slot, 0]))
            cps.append(pltpu.make_async_copy(ckrt_hbm.at[pg], krbuf.at[slot, :, pl.ds(p * page, page)], sem.at[slot, 1]))
        return cps

    @pl.when(s == 0)
    def _():
        for g0 in range(n_slots):
            for cp in chunk_copies(jnp.int32(g0), g0):
                cp.start()

    ql = ql_ref[0]
    qr = qr_ref[0].astype(BF16)

    newlat[...] = jnp.zeros(newlat.shape, F32)
    newkr[...] = jnp.zeros(newkr.shape, F32)
    newlat[0:dec_seq, :] = latn_ref[0]
    newkr[0:dec_seq, :] = krn_ref[0]
    nl = newlat[...].astype(BF16)
    sc = _dot_nt(ql, nl) + _dot_nt(qr, newkr[...].astype(BF16))
    r_i = lax.broadcasted_iota(jnp.int32, sc.shape, 0)
    j_i = lax.broadcasted_iota(jnp.int32, sc.shape, 1)
    sc = jnp.where(j_i * heads <= r_i, sc, -jnp.inf)
    rows = sc.shape[0]
    empty = (jnp.full((rows, 1), -jnp.inf, F32), jnp.zeros((rows, 1), F32), jnp.zeros((rows, nl.shape[1]), F32))
    init = (_softmax_step(sc, *empty, nl),) + (empty,) * (n_streams - 1)
    part = ppc * page // n_streams

    def body(c, carry):
        g = s * n_chunks + c
        slot = lax.rem(g, n_slots)
        for cp in chunk_copies(g, slot):
            cp.wait()
        lats = [latbuf[slot, pl.ds(st * part, part), :].astype(BF16) for st in range(n_streams)]
        scs = [_dot_nt(ql, lats[st]) + _dot(qr, krbuf[slot, :, pl.ds(st * part, part)].astype(BF16))
               for st in range(n_streams)]
        new = [_softmax_step(scs[st], *carry[st], lats[st]) for st in range(n_streams)]
        for cp in chunk_copies(g + n_slots, slot):
            cp.start()
        return tuple(new)

    carry = lax.fori_loop(0, n_chunks, body, init)

    @pl.when(s == nseq - 1)
    def _():
        for i in range(n_slots):
            for cp in chunk_copies(jnp.int32(total + i), (total + i) % n_slots):
                cp.wait()

    m = functools.reduce(jnp.maximum, [st[0] for st in carry])
    l = sum(st[1] * jnp.exp(st[0] - m) for st in carry)
    acc = sum(st[2] * jnp.exp(st[0] - m) for st in carry)
    o_ref[0] = (acc / l).astype(o_ref.dtype)


def _attn_sample(page_table, ql, qr, latn, krn, cache_lat, cache_krt, *, heads, n_slots=3, n_streams=2):
    nseq, n_pages = page_table.shape
    _, page, kvl = cache_lat.shape
    rope = cache_krt.shape[1]
    rows = ql.shape[1]
    dec_seq = latn.shape[1]
    pages_per_chunk = min(16, n_pages)
    assert n_pages % pages_per_chunk == 0
    n_chunks = n_pages // pages_per_chunk
    spare_rows = -(-n_slots // n_chunks)
    assert spare_rows <= nseq and nseq * n_chunks >= n_slots
    page_table = jnp.concatenate([page_table, page_table[:spare_rows]], axis=0)
    chunk = pages_per_chunk * page
    per_seq = lambda s, pt: (s, 0, 0)
    grid_spec = pltpu.PrefetchScalarGridSpec(
        num_scalar_prefetch=1,
        grid=(nseq,),
        in_specs=[pl.BlockSpec((1, rows, kvl), per_seq),
                  pl.BlockSpec((1, rows, rope), per_seq),
                  pl.BlockSpec((1, dec_seq, kvl), per_seq),
                  pl.BlockSpec((1, dec_seq, rope), per_seq),
                  pl.BlockSpec(memory_space=pl.ANY),
                  pl.BlockSpec(memory_space=pl.ANY)],
        out_specs=pl.BlockSpec((1, rows, kvl), per_seq),
        scratch_shapes=[pltpu.VMEM((n_slots, chunk, kvl), F32), pltpu.VMEM((n_slots, rope, chunk), F32),
                        pltpu.VMEM((LANES, kvl), F32), pltpu.VMEM((LANES, rope), F32),
                        pltpu.SemaphoreType.DMA((n_slots, 2))],
    )
    return pl.pallas_call(
        functools.partial(_attn_sample_kernel, nseq=nseq, pages_per_chunk=pages_per_chunk, n_chunks=n_chunks,
                          page=page, dec_seq=dec_seq, heads=heads, n_slots=n_slots, n_streams=n_streams),
        grid_spec=grid_spec,
        out_shape=jax.ShapeDtypeStruct((nseq, rows, kvl), BF16),
        compiler_params=_params(("arbitrary",), 44),
        name="attn_sample",
    )(page_table, ql, qr, latn, krn, cache_lat, cache_krt)


def _unabsorb_kernel(o_ref, wuv_ref, b_ref, *, heads):
    kvl = wuv_ref.shape[1]
    for hd in range(heads):
        b_ref[:, hd * V_HEAD:(hd + 1) * V_HEAD] = _dot(o_ref[:, hd * kvl:(hd + 1) * kvl], wuv_ref[hd]).astype(b_ref.dtype)


def _unabsorb(o_lat, wuv_h, *, heads, tm):
    n = o_lat.shape[0]
    return pl.pallas_call(
        functools.partial(_unabsorb_kernel, heads=heads),
        grid=(n // tm,),
        in_specs=[pl.BlockSpec((tm, o_lat.shape[1]), lambda i: (i, 0)),
                  _resident(wuv_h.shape, lambda i: (0, 0, 0))],
        out_specs=pl.BlockSpec((tm, heads * V_HEAD), lambda i: (i, 0)),
        out_shape=jax.ShapeDtypeStruct((n, heads * V_HEAD), BF16),
        compiler_params=_params(("arbitrary",), 32),
        name="unabsorb",
    )(o_lat, wuv_h)


def _outproj_kernel(*refs, widths):
    x_ref = refs[0]
    parts = refs[1:1 + len(widths)]
    w_ref = refs[1 + len(widths)]
    o_ref = refs[2 + len(widths)]
    acc = x_ref[...]
    off = 0
    for pr, w in zip(parts, widths):
        acc = acc + _dot(pr[...], w_ref[off:off + w, :])
        off += w
    o_ref[...] = acc


def _outproj(x, parts, w, *, tm):
    n, d = x.shape
    widths = tuple(p.shape[1] for p in parts)
    row = lambda i: (i, 0)
    return pl.pallas_call(
        functools.partial(_outproj_kernel, widths=widths),
        grid=(n // tm,),
        in_specs=[pl.BlockSpec((tm, d), row)] + [pl.BlockSpec((tm, wd), row) for wd in widths]
                 + [_resident(w.shape, lambda i: (0, 0))],
        out_specs=pl.BlockSpec((tm, d), row),
        out_shape=jax.ShapeDtypeStruct((n, d), F32),
        compiler_params=_params(("arbitrary",), 40),
        name="outproj",
    )(x, *parts, w)


def _mlp_kernel(x_ref, g_ref, wup_ref, wdn_ref, fg_ref, o_ref, xn_ref, *, final):
    j = pl.program_id(1)

    @pl.when(j == 0)
    def _():
        x = x_ref[...]
        xn_ref[...] = _rms(x, g_ref[...]).astype(BF16)
        o_ref[...] = x

    a = jnp.maximum(_dot(xn_ref[...], wup_ref[...]), 0.0)
    o_ref[...] += _dot((a * a).astype(BF16), wdn_ref[...])

    if final:
        @pl.when(j == pl.num_programs(1) - 1)
        def _():
            o_ref[...] = _rms(o_ref[...], fg_ref[...])


def _mlp(x, g, wup, wdn, fg, *, layer, final, tm, tf):
    n, d = x.shape
    dff = wup.shape[2]
    return pl.pallas_call(
        functools.partial(_mlp_kernel, final=final),
        grid=(n // tm, dff // tf),
        in_specs=[pl.BlockSpec((tm, d), lambda i, j: (i, 0)),
                  _resident((1, d), lambda i, j: (0, 0)),
                  pl.BlockSpec((None, d, tf), lambda i, j: (layer, 0, j)),
                  pl.BlockSpec((None, tf, d), lambda i, j: (layer, j, 0)),
                  _resident((1, d), lambda i, j: (0, 0))],
        out_specs=pl.BlockSpec((tm, d), lambda i, j: (i, 0)),
        out_shape=jax.ShapeDtypeStruct((n, d), F32),
        scratch_shapes=[pltpu.VMEM((tm, d), BF16)],
        compiler_params=_params(("arbitrary", "arbitrary"), 48),
        name="mlp_final" if final else "mlp",
    )(x, g, wup, wdn, fg)


def _inproj_cd_kernel(x_ref, g_ref, wv_ref, wg_ref, wb_ref, wc_ref, wh_ref, glu_ref, cgh_ref, bg_ref, xn_ref):
    j = pl.program_id(1)

    @pl.when(j == 0)
    def _():
        xn_ref[...] = _rms(x_ref[...], g_ref[...]).astype(BF16)

    xn = xn_ref[...]
    glu_ref[...] = _dot(xn, wv_ref[...]) * jax.nn.sigmoid(_dot(xn, wg_ref[...]))
    bg_ref[...] = _dot(xn, wb_ref[...])
    cgh_ref[...] = _dot(xn, wc_ref[...]) * _dot(xn, wh_ref[...])


def _inproj_cd(x, g, w, *, tm, tw):
    n, d = x.shape
    cdim = w.shape[1] // 5
    nct = cdim // tw
    out = jax.ShapeDtypeStruct((n, cdim), F32)
    ospec = pl.BlockSpec((tm, tw), lambda i, j: (i, j))
    return pl.pallas_call(
        _inproj_cd_kernel,
        grid=(n // tm, nct),
        in_specs=[pl.BlockSpec((tm, d), lambda i, j: (i, 0)),
                  _resident((1, d), lambda i, j: (0, 0))]
                 + [pl.BlockSpec((d, tw), lambda i, j, k=k: (0, k * nct + j)) for k in range(5)],
        out_specs=[ospec, ospec, ospec],
        out_shape=[out, out, out],
        scratch_shapes=[pltpu.VMEM((tm, d), BF16)],
        compiler_params=_params(("arbitrary", "arbitrary"), 48),
        name="inproj_cd",
    )(x, g, w, w, w, w, w)


def _ln_silu(y, g, b):
    mu = jnp.mean(y, axis=-1, keepdims=True)
    yc = y - mu
    var = jnp.mean(yc * yc, axis=-1, keepdims=True)
    z = yc * lax.rsqrt(var + EPS) * g + b
    return z * jax.nn.sigmoid(z)


def _conv_prompt_kernel(glu_ref, cgh_ref, bg_ref, cw_ref, cb_ref, lg_ref, lb_ref, sw_ref, o_ref,
                        ext_ref, sh_ref, ext2_ref, y_ref, *, tt, rc, lt):
    tb = pl.program_id(1)
    kw = cw_ref.shape[0]
    sk = sw_ref.shape[0]
    cdim = glu_ref.shape[1]
    sub = 8
    halo, halo2 = 32, 8

    @pl.when(tb == 0)
    def _():
        ext_ref[0:halo, :] = jnp.zeros((halo, cdim), F32)
        ext2_ref[0:halo2, :] = jnp.zeros((halo2, cdim), F32)

    ext_ref[halo:halo + tt, :] = glu_ref[...]
    ext2_ref[halo2:halo2 + tt, :] = cgh_ref[...]
    span = halo + tt - sub
    for j in range(1, sub):
        sh_ref[j - 1, 0:span, :] = ext_ref[j:j + span, :]

    for l0 in range(0, cdim, lt):
        ls = slice(l0, l0 + lt)
        for r0 in range(0, tt, rc):
            acc = jnp.zeros((rc, lt), F32)
            for k in range(kw):
                off = halo - (kw - 1) + k
                j, base = off % sub, r0 + off - off % sub
                rows = ext_ref[base:base + rc, ls] if j == 0 else sh_ref[j - 1, base:base + rc, ls]
                acc = acc + rows * cw_ref[k:k + 1, ls]
            y_ref[r0:r0 + rc, ls] = acc + cb_ref[:, ls]
            sc = jnp.zeros((rc, lt), F32)
            for k in range(sk):
                base = halo2 - (sk - 1) + r0 + k
                sc = sc + ext2_ref[base:base + rc, ls] * sw_ref[k:k + 1, ls]
            o_ref[r0:r0 + rc, cdim + l0:cdim + l0 + lt] = (bg_ref[r0:r0 + rc, ls] * sc).astype(o_ref.dtype)

    o_ref[:, 0:cdim] = _ln_silu(y_ref[...], lg_ref[...], lb_ref[...]).astype(o_ref.dtype)

    ext_ref[0:halo, :] = ext_ref[tt:tt + halo, :]
    ext2_ref[0:halo2, :] = ext2_ref[tt:tt + halo2, :]


def _conv_prompt(glu, cgh, bg, cw, cb, lg, lb, sw, *, seq, tt=256):
    n, cdim = glu.shape
    nt = seq // tt
    blk = lambda b, t: (b * nt + t, 0)
    c2 = lambda b, t: (0, 0)
    return pl.pallas_call(
        functools.partial(_conv_prompt_kernel, tt=tt, rc=tt // 2, lt=LANES),
        grid=(n // seq, nt),
        in_specs=[pl.BlockSpec((tt, cdim), blk), pl.BlockSpec((tt, cdim), blk), pl.BlockSpec((tt, cdim), blk),
                  _resident(cw.shape, c2), _resident((1, cdim), c2), _resident((1, cdim), c2),
                  _resident((1, cdim), c2), _resident(sw.shape, c2)],
        out_specs=pl.BlockSpec((tt, 2 * cdim), blk),
        out_shape=jax.ShapeDtypeStruct((n, 2 * cdim), BF16),
        scratch_shapes=[pltpu.VMEM((tt + 32, cdim), F32), pltpu.VMEM((7, tt + 24, cdim), F32),
                        pltpu.VMEM((tt + 8, cdim), F32), pltpu.VMEM((tt, cdim), F32)],
        compiler_params=_params(("arbitrary", "arbitrary"), 40),
        name="conv_prompt",
    )(glu, cgh, bg, cw, cb, lg, lb, sw)


def _conv_sample_kernel(cst_ref, sst_ref, glu_ref, cgh_ref, bg_ref, cw_ref, cb_ref, lg_ref, lb_ref, sw_ref, o_ref,
                        *, dec_seq):
    kw = cw_ref.shape[0]
    sk = sw_ref.shape[0]
    cdim = glu_ref.shape[2]

    def ext(r):
        return cst_ref[r] if r < kw - 1 else glu_ref[r - (kw - 1)]

    def ext2(r):
        return sst_ref[r] if r < sk - 1 else cgh_ref[r - (sk - 1)]

    for t in range(dec_seq):
        acc = ext(t) * cw_ref[0:1, :]
        for k in range(1, kw):
            acc = acc + ext(t + k) * cw_ref[k:k + 1, :]
        o_ref[t, :, 0:cdim] = _ln_silu(acc + cb_ref[...], lg_ref[...], lb_ref[...]).astype(o_ref.dtype)
        sc = ext2(t) * sw_ref[0:1, :]
        for k in range(1, sk):
            sc = sc + ext2(t + k) * sw_ref[k:k + 1, :]
        o_ref[t, :, cdim:2 * cdim] = (bg_ref[t] * sc).astype(o_ref.dtype)


def _conv_sample(cst, sst, glu, cgh, bg, cw, cb, lg, lb, sw, *, tb=32):
    t, b, cdim = glu.shape
    blk = lambda i: (0, i, 0)
    c2 = lambda i: (0, 0)
    new = pl.BlockSpec((t, tb, cdim), blk)
    return pl.pallas_call(
        functools.partial(_conv_sample_kernel, dec_seq=t),
        grid=(b // tb,),
        in_specs=[pl.BlockSpec((cst.shape[0], tb, cdim), blk), pl.BlockSpec((sst.shape[0], tb, cdim), blk),
                  new, new, new,
                  _resident(cw.shape, c2), _resident((1, cdim), c2), _resident((1, cdim), c2),
                  _resident((1, cdim), c2), _resident(sw.shape, c2)],
        out_specs=pl.BlockSpec((t, tb, 2 * cdim), blk),
        out_shape=jax.ShapeDtypeStruct((t, b, 2 * cdim), BF16),
        compiler_params=_params(("arbitrary",), 32),
        name="conv_sample",
    )(cst, sst, glu, cgh, bg, cw, cb, lg, lb, sw)


def _rope_tables(pos):
    half = QK_ROPE // 2
    inv = ROPE_THETA ** (-jnp.arange(half, dtype=F32) / half)
    ang = pos.astype(F32)[:, None] * inv[None, :]
    z = jnp.zeros((pos.shape[0], LANES - QK_ROPE), F32)
    cos, sin = jnp.cos(ang), jnp.sin(ang)
    return jnp.concatenate([cos, cos, z], axis=-1), jnp.concatenate([sin, sin, z], axis=-1)


def _rot_cols(w):
    half = QK_ROPE // 2
    return jnp.concatenate([-w[..., half:], w[..., :half]], axis=-1)


def _pad_cols(w, width):
    return jnp.concatenate([w, jnp.zeros(w.shape[:-1] + (width - w.shape[-1],), w.dtype)], axis=-1)


def _tail(buf, new, keep):
    new_tail = new[:, max(0, new.shape[1] - keep):]
    return jnp.concatenate([buf.astype(new.dtype), new_tail], axis=1)[:, -keep:]


def kernel(x_prompt, x_sample, cache_mla_latent, cache_mla_krope, page_table, state_pool, state_ccv, state_sconv,
           norm1_g, norm2_g, final_norm_g, w_in_ab, pool_w, pool_scale, q_norm_g, w_uq, kv_norm_g, w_uk, w_uv,
           w_out_ab, w_in_cd, ccv_w, ccv_b, ccv_ln_g, ccv_ln_b, sconv_w, w_out_cd, w_up, w_down):
    bsz, seq, d = x_prompt.shape
    dbsz, dseq, _ = x_sample.shape
    depth = norm1_g.shape[0]
    page = cache_mla_latent.shape[2]
    past_len = page_table.shape[1] * page
    heads = w_uq.shape[2]
    pd = state_pool.shape[-1]
    ql = q_norm_g.shape[-1]
    kvl = kv_norm_g.shape[-1]
    cdim = state_ccv.shape[-1]
    n_p, n_s = bsz * seq, dbsz * dseq
    tm_p = 256 if n_p % 256 == 0 else n_p
    tm_s = 256 if n_s % 256 == 0 else n_s
    tmm_p = 512 if n_p % 512 == 0 else tm_p
    tmm_s = 512 if n_s % 512 == 0 else tm_s

    hp = x_prompt.reshape(n_p, d)
    hs = x_sample.transpose(1, 0, 2).reshape(n_s, d)

    cos_p, sin_p = _rope_tables(jnp.arange(seq, dtype=jnp.int32))
    pos_s = jnp.repeat(past_len + jnp.arange(dseq, dtype=jnp.int32), dbsz)
    cos_s, sin_s = _rope_tables(pos_s)

    def row2(v):
        return v.reshape(1, -1)

    def to_bt(a):
        return a.reshape(dseq, dbsz, -1).transpose(1, 0, 2)

    wup = w_up.astype(BF16)
    wdn = w_down.astype(BF16)
    fg = row2(final_norm_g)
    p_lat, p_kr, s_lat, s_kr, p_pool, s_pool = [], [], [], [], [], []
    p_ccv, s_ccv, p_sc, s_sc = [], [], [], []
    for layer in range(depth):
        i = layer // 2
        last = layer == depth - 1
        if layer % 2 == 0:
            wi = w_in_ab[i]
            o3 = pd + ql + kvl
            k_r = wi[:, o3:]
            win = jnp.concatenate([wi[:, :o3], _pad_cols(k_r, LANES), _pad_cols(_rot_cols(k_r), LANES)], axis=-1).astype(BF16)
            wq_h = w_uq[i]
            wq = _pad_cols(wq_h, HEAD_PAD).reshape(ql, heads * HEAD_PAD).astype(BF16)
            wqr = _pad_cols(_rot_cols(wq_h[..., QK_NOPE:]), LANES).reshape(ql, heads * LANES).astype(BF16)
            wuk2 = w_uk[i].reshape(kvl, heads * QK_NOPE).astype(BF16)
            wuv2 = w_uv[i].reshape(kvl, heads * V_HEAD).astype(BF16)
            wuk_t = w_uk[i].transpose(1, 2, 0).astype(BF16)
            wuv_h = w_uv[i].transpose(1, 0, 2).astype(BF16)
            pw = pool_w[i].astype(BF16)
            ps = row2(pool_scale[i])
            wo = w_out_ab[i].astype(BF16)
            common = (row2(norm1_g[layer]), win, row2(q_norm_g[i]), row2(kv_norm_g[i]), wq, wqr)

            u, lat, kr, q, k, v = _inproj_ab(hp, *common, cos_p, sin_p, (wuk2, wuv2), heads=heads, sample=False,
                                             tm=tm_p, table_period=seq)
            a_out = _pool_prompt(u, pw, ps, seq=seq)
            b_out = _attn_prompt(q, k, v, seq=seq, heads=heads)
            hp = _outproj(hp, [a_out, b_out], wo, tm=tmm_p)
            p_lat.append(lat.reshape(bsz, seq, kvl))
            p_kr.append(kr.reshape(bsz, seq, QK_ROPE))
            p_pool.append(_tail(jnp.zeros((bsz, POOL_BUF, pd), F32), u.reshape(bsz, seq, pd), POOL_BUF))

            u, lat, kr, qlat, qrope = _inproj_ab(hs, *common, cos_s, sin_s, (wuk_t,), heads=heads, sample=True,
                                                 tm=tm_s, table_period=n_s)
            a_out = _pool_sample(state_pool[i].transpose(1, 0, 2), u.reshape(dseq, dbsz, pd), pw, ps, pos0=past_len)
            ql_b = qlat.reshape(dseq, dbsz, heads, kvl).transpose(1, 0, 2, 3).reshape(dbsz, dseq * heads, kvl)
            qr_b = (qrope.reshape(dseq, dbsz, heads, LANES)[..., :QK_ROPE]
                    .transpose(1, 0, 2, 3).reshape(dbsz, dseq * heads, QK_ROPE))
            lat_b, kr_b = to_bt(lat), to_bt(kr)
            o_lat = _attn_sample(page_table, ql_b, qr_b, lat_b, kr_b, cache_mla_latent[i],
                                 jnp.swapaxes(cache_mla_krope[i], 1, 2), heads=heads)
            o_tm = o_lat.reshape(dbsz, dseq, heads * kvl).transpose(1, 0, 2).reshape(n_s, heads * kvl)
            b_out = _unabsorb(o_tm, wuv_h, heads=heads, tm=tm_s)
            hs = _outproj(hs, [a_out.reshape(n_s, pd), b_out], wo, tm=tmm_s)
            s_lat.append(lat_b)
            s_kr.append(kr_b)
            s_pool.append(_tail(state_pool[i], to_bt(u), POOL_BUF))
        else:
            tw = 512
            wcd = w_in_cd[i].astype(BF16)
            conv_w = (ccv_w[i], row2(ccv_b[i]), row2(ccv_ln_g[i]), row2(ccv_ln_b[i]), sconv_w[i])
            wo = w_out_cd[i].astype(BF16)
            kw, sk = ccv_w.shape[1], sconv_w.shape[1]

            glu, cgh, bg = _inproj_cd(hp, row2(norm1_g[layer]), wcd, tm=tmm_p, tw=tw)
            cd = _conv_prompt(glu, cgh, bg, *conv_w, seq=seq)
            hp = _outproj(hp, [cd], wo, tm=tmm_p)
            p_ccv.append(_tail(jnp.zeros((bsz, kw - 1, cdim), F32), glu.reshape(bsz, seq, cdim), kw - 1))
            p_sc.append(_tail(jnp.zeros((bsz, sk - 1, cdim), F32), cgh.reshape(bsz, seq, cdim), sk - 1))

            glu, cgh, bg = _inproj_cd(hs, row2(norm1_g[layer]), wcd, tm=tmm_s, tw=tw)
            cd = _conv_sample(state_ccv[i].transpose(1, 0, 2), state_sconv[i].transpose(1, 0, 2),
                              glu.reshape(dseq, dbsz, cdim), cgh.reshape(dseq, dbsz, cdim),
                              bg.reshape(dseq, dbsz, cdim), *conv_w)
            hs = _outproj(hs, [cd.reshape(n_s, 2 * cdim)], wo, tm=tmm_s)
            s_ccv.append(_tail(state_ccv[i], to_bt(glu), kw - 1))
            s_sc.append(_tail(state_sconv[i], to_bt(cgh), sk - 1))

        hp = _mlp(hp, row2(norm2_g[layer]), wup, wdn, fg, layer=layer, final=last, tm=tmm_p, tf=1024)
        hs = _mlp(hs, row2(norm2_g[layer]), wup, wdn, fg, layer=layer, final=last, tm=tmm_s, tf=1024)

    y_prompt = hp.reshape(bsz, seq, d)
    y_sample = to_bt(hs)
    return (y_prompt, y_sample,
            jnp.stack(p_lat), jnp.stack(p_kr), jnp.stack(s_lat), jnp.stack(s_kr),
            jnp.stack(p_pool), jnp.stack(s_pool),
            jnp.stack(p_ccv), jnp.stack(s_ccv),
            jnp.stack(p_sc), jnp.stack(s_sc))
```

```python
import functools

import jax
import jax.numpy as jnp
from jax import lax
from jax.experimental import pallas as pl
from jax.experimental.pallas import tpu as pltpu

F32 = jnp.float32
BF16 = jnp.bfloat16

EPS = 1e-6
POOL_WINDOWS = (2, 4, 8, 16)
POOL_BUF = max(POOL_WINDOWS) - 1
LANES = 128
HEAD_PAD = 256
QK_NOPE = 128
QK_ROPE = 64
V_HEAD = 128
ROPE_THETA = 10000.0
MLA_SCALE = (QK_NOPE + QK_ROPE) ** -0.5
MIB = 1024 * 1024


def _params(semantics, vmem_mib):
    return pltpu.CompilerParams(dimension_semantics=semantics, vmem_limit_bytes=vmem_mib * MIB)


def _resident(shape, index_map):
    return pl.BlockSpec(shape, index_map, pipeline_mode=pl.Buffered(1))


def _rms(x, g):
    ms = jnp.mean(x * x, axis=-1, keepdims=True)
    return x * lax.rsqrt(ms + EPS) * g


def _dot(a, b):
    return jnp.dot(a, b, preferred_element_type=F32)


def _dot_nt(a, b):
    return lax.dot_general(a, b, (((1,), (1,)), ((), ())), preferred_element_type=F32)


def _inproj_ab_kernel(x_ref, g1_ref, win_ref, qg_ref, kvg_ref, wq_ref, wqr_ref, cos_ref, sin_ref, *rest,
                      heads, sample):
    if sample:
        wukt_ref, u_ref, lat_ref, kr_ref, qlat_ref, qrope_ref = rest
    else:
        wuk_ref, wuv_ref, u_ref, lat_ref, kr_ref, q_ref, k_ref, v_ref = rest
    pd = u_ref.shape[1]
    ql = qg_ref.shape[1]
    kvl = kvg_ref.shape[1]
    h = _rms(x_ref[...], g1_ref[...]).astype(BF16)
    z = _dot(h, win_ref[...])
    o1, o2, o3 = pd, pd + ql, pd + ql + kvl
    u_ref[...] = z[:, :o1]
    lat = _rms(z[:, o2:o3], kvg_ref[...])
    lat_ref[...] = lat
    cos = cos_ref[...]
    sin = sin_ref[...]
    krope = z[:, o3:o3 + LANES] * cos + z[:, o3 + LANES:o3 + 2 * LANES] * sin
    kr_ref[...] = krope[:, :QK_ROPE]
    cqn = _rms(z[:, o1:o2], qg_ref[...]).astype(BF16)
    qm = _dot(cqn, wq_ref[...])
    qrot = _dot(cqn, wqr_ref[...])
    if not sample:
        latb = lat.astype(BF16)
        kn = _dot(latb, wuk_ref[...])
        v_ref[...] = _dot(latb, wuv_ref[...]).astype(BF16)
        krb = krope.astype(BF16)
    for hd in range(heads):
        c0 = hd * HEAD_PAD
        nope = qm[:, c0:c0 + QK_NOPE] * MLA_SCALE
        rope = (qm[:, c0 + QK_NOPE:c0 + HEAD_PAD] * cos + qrot[:, hd * LANES:(hd + 1) * LANES] * sin) * MLA_SCALE
        if sample:
            qlat_ref[:, hd * kvl:(hd + 1) * kvl] = _dot(nope.astype(BF16), wukt_ref[hd]).astype(BF16)
            qrope_ref[:, hd * LANES:(hd + 1) * LANES] = rope
        else:
            q_ref[:, c0:c0 + QK_NOPE] = nope.astype(BF16)
            q_ref[:, c0 + QK_NOPE:c0 + HEAD_PAD] = rope.astype(BF16)
            k_ref[:, c0:c0 + QK_NOPE] = kn[:, hd * QK_NOPE:(hd + 1) * QK_NOPE].astype(BF16)
            k_ref[:, c0 + QK_NOPE:c0 + HEAD_PAD] = krb


def _inproj_ab(x, g1, win, qg, kvg, wq, wqr, cos, sin, extra_w, *, heads, sample, tm, table_period):
    n, d = x.shape
    ql, kvl = qg.shape[1], kvg.shape[1]
    pd = win.shape[1] - ql - kvl - 2 * LANES
    nt = table_period // tm
    row = lambda i: (i, 0)
    const = lambda i: (0, 0)
    in_specs = [
        pl.BlockSpec((tm, d), row),
        _resident((1, d), const),
        _resident(win.shape, const),
        _resident((1, ql), const),
        _resident((1, kvl), const),
        _resident(wq.shape, const),
        _resident(wqr.shape, const),
        pl.BlockSpec((tm, LANES), lambda i: (i % nt, 0)),
        pl.BlockSpec((tm, LANES), lambda i: (i % nt, 0)),
    ] + [_resident(w.shape, (lambda i: (0, 0, 0)) if w.ndim == 3 else const) for w in extra_w]
    out_shape = [jax.ShapeDtypeStruct((n, pd), F32), jax.ShapeDtypeStruct((n, kvl), F32),
                 jax.ShapeDtypeStruct((n, QK_ROPE), F32)]
    out_specs = [pl.BlockSpec((tm, pd), row), pl.BlockSpec((tm, kvl), row), pl.BlockSpec((tm, QK_ROPE), row)]
    if sample:
        out_shape += [jax.ShapeDtypeStruct((n, heads * kvl), BF16), jax.ShapeDtypeStruct((n, heads * LANES), F32)]
        out_specs += [pl.BlockSpec((tm, heads * kvl), row), pl.BlockSpec((tm, heads * LANES), row)]
    else:
        out_shape += [jax.ShapeDtypeStruct((n, heads * HEAD_PAD), BF16), jax.ShapeDtypeStruct((n, heads * HEAD_PAD), BF16),
                      jax.ShapeDtypeStruct((n, heads * V_HEAD), BF16)]
        out_specs += [pl.BlockSpec((tm, heads * HEAD_PAD), row), pl.BlockSpec((tm, heads * HEAD_PAD), row),
                      pl.BlockSpec((tm, heads * V_HEAD), row)]
    return pl.pallas_call(
        functools.partial(_inproj_ab_kernel, heads=heads, sample=sample),
        grid=(n // tm,), in_specs=in_specs, out_specs=out_specs, out_shape=out_shape,
        compiler_params=_params(("arbitrary",), 52),
        name="inproj_ab_sample" if sample else "inproj_ab_prompt",
    )(x, g1, win, qg, kvg, wq, wqr, cos, sin, *extra_w)


def _pool_mix(pooled, pw_ref, ps_ref):
    outs = []
    for g in range(len(POOL_WINDOWS)):
        sl = slice(g * LANES, (g + 1) * LANES)
        outs.append(_dot(pooled[:, sl].astype(BF16), pw_ref[g]))
    return jnp.concatenate(outs, axis=-1) * ps_ref[...]


def _pool_prompt_kernel(u_ref, pw_ref, ps_ref, o_ref, ext_ref, *, seq, tc):
    halo = 16
    ext_ref[0:halo, :] = jnp.zeros((halo, ext_ref.shape[1]), F32)
    ext_ref[halo:halo + seq, :] = u_ref[...]
    for r0 in range(0, seq, tc):
        pos = r0 + lax.broadcasted_iota(jnp.int32, (tc, 1), 0)
        cols = []
        for g, w in enumerate(POOL_WINDOWS):
            sl = slice(g * LANES, (g + 1) * LANES)
            cur = ext_ref[halo + r0:halo + r0 + tc, sl]
            s = cur
            for j in range(1, w):
                s = s + ext_ref[halo + r0 - j:halo + r0 - j + tc, sl]
            cnt = jnp.minimum(pos + 1, w).astype(F32)
            cols.append(s / cnt - cur)
        pooled = jnp.concatenate(cols, axis=-1)
        o_ref[r0:r0 + tc, :] = _pool_mix(pooled, pw_ref, ps_ref).astype(o_ref.dtype)


def _pool_prompt(u, pw, ps, *, seq):
    n, pd = u.shape
    return pl.pallas_call(
        functools.partial(_pool_prompt_kernel, seq=seq, tc=256),
        grid=(n // seq,),
        in_specs=[pl.BlockSpec((seq, pd), lambda b: (b, 0)),
                  _resident(pw.shape, lambda b: (0, 0, 0)),
                  _resident((1, pd), lambda b: (0, 0))],
        out_specs=pl.BlockSpec((seq, pd), lambda b: (b, 0)),
        out_shape=jax.ShapeDtypeStruct((n, pd), BF16),
        scratch_shapes=[pltpu.VMEM((seq + 16, pd), F32)],
        compiler_params=_params(("arbitrary",), 32),
        name="pool_prompt",
    )(u, pw, ps)


def _pool_sample_kernel(st_ref, u_ref, pw_ref, ps_ref, o_ref, *, dec_seq, pos0):
    nb = st_ref.shape[0]

    def row(r):
        return (lambda sl: st_ref[r, :, sl]) if r < nb else (lambda sl: u_ref[r - nb, :, sl])

    for t in range(dec_seq):
        cols = []
        for g, w in enumerate(POOL_WINDOWS):
            sl = slice(g * LANES, (g + 1) * LANES)
            cur = row(nb + t)(sl)
            s = cur
            for j in range(1, w):
                s = s + row(nb + t - j)(sl)
            cnt = float(min(pos0 + t + 1, w))
            cols.append(s / cnt - cur)
        pooled = jnp.concatenate(cols, axis=-1)
        o_ref[t] = _pool_mix(pooled, pw_ref, ps_ref).astype(o_ref.dtype)


def _pool_sample(st, u, pw, ps, *, pos0):
    t, b, pd = u.shape
    full3 = lambda i: (0, 0, 0)
    return pl.pallas_call(
        functools.partial(_pool_sample_kernel, dec_seq=t, pos0=pos0),
        grid=(1,),
        in_specs=[pl.BlockSpec(st.shape, full3), pl.BlockSpec(u.shape, full3),
                  pl.BlockSpec(pw.shape, full3), pl.BlockSpec((1, pd), lambda i: (0, 0))],
        out_specs=pl.BlockSpec((t, b, pd), full3),
        out_shape=jax.ShapeDtypeStruct((t, b, pd), BF16),
        compiler_params=_params(("arbitrary",), 32),
        name="pool_sample",
    )(st, u, pw, ps)


def _softmax_step(s, m, l, acc, vv):
    m_new = jnp.maximum(m, jnp.max(s, axis=-1, keepdims=True))
    alpha = jnp.exp(m - m_new)
    p = jnp.exp(s - m_new)
    l = alpha * l + jnp.sum(p, axis=-1, keepdims=True)
    acc = alpha * acc + _dot(p.astype(BF16), vv)
    return m_new, l, acc


def _attn_prompt_kernel(q_ref, k_ref, v_ref, o_ref, *, tq, heads, heads_per_pass):
    qi = pl.program_id(1)
    causal = (lax.broadcasted_iota(jnp.int32, (tq, tq), 1) <= lax.broadcasted_iota(jnp.int32, (tq, tq), 0))
    for h0 in range(0, heads, heads_per_pass):
        hds = tuple(range(h0, h0 + heads_per_pass))
        qs = [q_ref[:, hd * HEAD_PAD:(hd + 1) * HEAD_PAD] for hd in hds]

        def step(k0, carry, masked, hds=hds, qs=qs):
            scs = [_dot_nt(q, k_ref[pl.ds(k0, tq), hd * HEAD_PAD:(hd + 1) * HEAD_PAD]) for q, hd in zip(qs, hds)]
            if masked:
                scs = [jnp.where(causal, s, -jnp.inf) for s in scs]
            return tuple(_softmax_step(s, m, l, acc, v_ref[pl.ds(k0, tq), hd * V_HEAD:(hd + 1) * V_HEAD])
                         for s, hd, (m, l, acc) in zip(scs, hds, carry))

        init = tuple((jnp.full((tq, 1), -jnp.inf, F32), jnp.zeros((tq, 1), F32), jnp.zeros((tq, V_HEAD), F32))
                     for _ in hds)
        carry = lax.fori_loop(0, qi, lambda j, c: step(pl.multiple_of(j * tq, tq), c, False), init)
        carry = step(pl.multiple_of(qi * tq, tq), carry, True)
        for hd, (_, l, acc) in zip(hds, carry):
            o_ref[:, hd * V_HEAD:(hd + 1) * V_HEAD] = (acc / l).astype(o_ref.dtype)


def _attn_prompt(q, k, v, *, seq, heads, tq=512, heads_per_pass=2):
    n = q.shape[0]
    tq = min(tq, seq)
    nq = seq // tq
    return pl.pallas_call(
        functools.partial(_attn_prompt_kernel, tq=tq, heads=heads, heads_per_pass=heads_per_pass),
        grid=(n // seq, nq),
        in_specs=[pl.BlockSpec((tq, heads * HEAD_PAD), lambda b, i: (b * nq + i, 0)),
                  _resident((seq, heads * HEAD_PAD), lambda b, i: (b, 0)),
                  _resident((seq, heads * V_HEAD), lambda b, i: (b, 0))],
        out_specs=pl.BlockSpec((tq, heads * V_HEAD), lambda b, i: (b * nq + i, 0)),
        out_shape=jax.ShapeDtypeStruct((n, heads * V_HEAD), BF16),
        compiler_params=_params(("arbitrary", "arbitrary"), 48),
        name="attn_prompt",
    )(q, k, v)


def _attn_sample_kernel(pt_ref, ql_ref, qr_ref, latn_ref, krn_ref, clat_hbm, ckrt_hbm, o_ref,
                        latbuf, krbuf, newlat, newkr, sem, *, nseq, pages_per_chunk, n_chunks, page, dec_seq, heads,
                        n_slots, n_streams):
    s = pl.program_id(0)
    total = nseq * n_chunks
    ppc = pages_per_chunk

    def chunk_copies(g, slot):
        seq = lax.div(g, n_chunks)
        c = lax.rem(g, n_chunks)
        cps = []
        for p in range(ppc):
            pg = pt_ref[seq, c * ppc + p]
            cps.append(pltpu.make_async_copy(clat_hbm.at[pg], latbuf.at[slot, pl.ds(p * page, page)], sem.at[slot, 0]))
            cps.append(pltpu.make_async_copy(ckrt_hbm.at[pg], krbuf.at[slot, :, pl.ds(p * page, page)], sem.at[slot, 1]))
        return cps

    @pl.when(s == 0)
    def _():
        for g0 in range(n_slots):
            for cp in chunk_copies(jnp.int32(g0), g0):
                cp.start()

    ql = ql_ref[0]
    qr = qr_ref[0].astype(BF16)

    newlat[...] = jnp.zeros(newlat.shape, F32)
    newkr[...] = jnp.zeros(newkr.shape, F32)
    newlat[0:dec_seq, :] = latn_ref[0]
    newkr[0:dec_seq, :] = krn_ref[0]
    nl = newlat[...].astype(BF16)
    sc = _dot_nt(ql, nl) + _dot_nt(qr, newkr[...].astype(BF16))
    r_i = lax.broadcasted_iota(jnp.int32, sc.shape, 0)
    j_i = lax.broadcasted_iota(jnp.int32, sc.shape, 1)
    sc = jnp.where(j_i * heads <= r_i, sc, -jnp.inf)
    rows = sc.shape[0]
    empty = (jnp.full((rows, 1), -jnp.inf, F32), jnp.zeros((rows, 1), F32), jnp.zeros((rows, nl.shape[1]), F32))
    init = (_softmax_step(sc, *empty, nl),) + (empty,) * (n_streams - 1)
    part = ppc * page // n_streams

    def body(c, carry):
        g = s * n_chunks + c
        slot = lax.rem(g, n_slots)
        for cp in chunk_copies(g, slot):
            cp.wait()
        lats = [latbuf[slot, pl.ds(st * part, part), :].astype(BF16) for st in range(n_streams)]
        scs = [_dot_nt(ql, lats[st]) + _dot(qr, krbuf[slot, :, pl.ds(st * part, part)].astype(BF16))
               for st in range(n_streams)]
        new = [_softmax_step(scs[st], *carry[st], lats[st]) for st in range(n_streams)]
        for cp in chunk_copies(g + n_slots, slot):
            cp.start()
        return tuple(new)

    carry = lax.fori_loop(0, n_chunks, body, init)

    @pl.when(s == nseq - 1)
    def _():
        for i in range(n_slots):
            for cp in chunk_copies(jnp.int32(total + i), (total + i) % n_slots):
                cp.wait()

    m = functools.reduce(jnp.maximum, [st[0] for st in carry])
    l = sum(st[1] * jnp.exp(st[0] - m) for st in carry)
    acc = sum(st[2] * jnp.exp(st[0] - m) for st in carry)
    o_ref[0] = (acc / l).astype(o_ref.dtype)


def _attn_sample(page_table, ql, qr, latn, krn, cache_lat, cache_krt, *, heads, n_slots=3, n_streams=2):
    nseq, n_pages = page_table.shape
    _, page, kvl = cache_lat.shape
    rope = cache_krt.shape[1]
    rows = ql.shape[1]
    dec_seq = latn.shape[1]
    pages_per_chunk = min(16, n_pages)
    assert n_pages % pages_per_chunk == 0
    n_chunks = n_pages // pages_per_chunk
    spare_rows = -(-n_slots // n_chunks)
    assert spare_rows <= nseq and nseq * n_chunks >= n_slots
    page_table = jnp.concatenate([page_table, page_table[:spare_rows]], axis=0)
    chunk = pages_per_chunk * page
    per_seq = lambda s, pt: (s, 0, 0)
    grid_spec = pltpu.PrefetchScalarGridSpec(
        num_scalar_prefetch=1,
        grid=(nseq,),
        in_specs=[pl.BlockSpec((1, rows, kvl), per_seq),
                  pl.BlockSpec((1, rows, rope), per_seq),
                  pl.BlockSpec((1, dec_seq, kvl), per_seq),
                  pl.BlockSpec((1, dec_seq, rope), per_seq),
                  pl.BlockSpec(memory_space=pl.ANY),
                  pl.BlockSpec(memory_space=pl.ANY)],
        out_specs=pl.BlockSpec((1, rows, kvl), per_seq),
        scratch_shapes=[pltpu.VMEM((n_slots, chunk, kvl), F32), pltpu.VMEM((n_slots, rope, chunk), F32),
                        pltpu.VMEM((LANES, kvl), F32), pltpu.VMEM((LANES, rope), F32),
                        pltpu.SemaphoreType.DMA((n_slots, 2))],
    )
    return pl.pallas_call(
        functools.partial(_attn_sample_kernel, nseq=nseq, pages_per_chunk=pages_per_chunk, n_chunks=n_chunks,
                          page=page, dec_seq=dec_seq, heads=heads, n_slots=n_slots, n_streams=n_streams),
        grid_spec=grid_spec,
        out_shape=jax.ShapeDtypeStruct((nseq, rows, kvl), BF16),
        compiler_params=_params(("arbitrary",), 44),
        name="attn_sample",
    )(page_table, ql, qr, latn, krn, cache_lat, cache_krt)


def _unabsorb_kernel(o_ref, wuv_ref, b_ref, *, heads):
    kvl = wuv_ref.shape[1]
    for hd in range(heads):
        b_ref[:, hd * V_HEAD:(hd + 1) * V_HEAD] = _dot(o_ref[:, hd * kvl:(hd + 1) * kvl], wuv_ref[hd]).astype(b_ref.dtype)


def _unabsorb(o_lat, wuv_h, *, heads, tm):
    n = o_lat.shape[0]
    return pl.pallas_call(
        functools.partial(_unabsorb_kernel, heads=heads),
        grid=(n // tm,),
        in_specs=[pl.BlockSpec((tm, o_lat.shape[1]), lambda i: (i, 0)),
                  _resident(wuv_h.shape, lambda i: (0, 0, 0))],
        out_specs=pl.BlockSpec((tm, heads * V_HEAD), lambda i: (i, 0)),
        out_shape=jax.ShapeDtypeStruct((n, heads * V_HEAD), BF16),
        compiler_params=_params(("arbitrary",), 32),
        name="unabsorb",
    )(o_lat, wuv_h)


def _mlp_kernel(*refs, n_parts, final):
    x_ref = refs[0]
    parts = refs[1:1 + n_parts]
    wo_ref, g_ref, wup_ref, wdn_ref, fg_ref, o_ref, xn_ref = refs[1 + n_parts:]
    j = pl.program_id(1)

    @pl.when(j == 0)
    def _():
        x = x_ref[...]
        off = 0
        for pr in parts:
            x = x + _dot(pr[...], wo_ref[off:off + pr.shape[1], :])
            off += pr.shape[1]
        xn_ref[...] = _rms(x, g_ref[...]).astype(BF16)
        o_ref[...] = x

    a = jnp.maximum(_dot(xn_ref[...], wup_ref[...]), 0.0)
    o_ref[...] += _dot((a * a).astype(BF16), wdn_ref[...])

    if final:
        @pl.when(j == pl.num_programs(1) - 1)
        def _():
            o_ref[...] = _rms(o_ref[...], fg_ref[...])


def _mix_mlp(x, parts, wo, g, wup, wdn, fg, *, layer, final, tm, tf):
    n, d = x.shape
    dff = wup.shape[2]
    row = lambda i, j: (i, 0)
    return pl.pallas_call(
        functools.partial(_mlp_kernel, n_parts=len(parts), final=final),
        grid=(n // tm, dff // tf),
        in_specs=[pl.BlockSpec((tm, d), row)]
                 + [pl.BlockSpec((tm, p.shape[1]), row, pipeline_mode=pl.Buffered(1)) for p in parts]
                 + [_resident(wo.shape, lambda i, j: (0, 0)),
                  _resident((1, d), lambda i, j: (0, 0)),
                  pl.BlockSpec((None, d, tf), lambda i, j: (layer, 0, j)),
                  pl.BlockSpec((None, tf, d), lambda i, j: (layer, j, 0)),
                  _resident((1, d), lambda i, j: (0, 0))],
        out_specs=pl.BlockSpec((tm, d), lambda i, j: (i, 0)),
        out_shape=jax.ShapeDtypeStruct((n, d), F32),
        scratch_shapes=[pltpu.VMEM((tm, d), BF16)],
        compiler_params=_params(("arbitrary", "arbitrary"), 58),
        name="mlp_final" if final else "mlp",
    )(x, *parts, wo, g, wup, wdn, fg)


def _inproj_cd_kernel(x_ref, g_ref, wv_ref, wg_ref, wb_ref, wc_ref, wh_ref, glu_ref, cgh_ref, bg_ref, xn_ref):
    j = pl.program_id(1)

    @pl.when(j == 0)
    def _():
        xn_ref[...] = _rms(x_ref[...], g_ref[...]).astype(BF16)

    xn = xn_ref[...]
    glu_ref[...] = _dot(xn, wv_ref[...]) * jax.nn.sigmoid(_dot(xn, wg_ref[...]))
    bg_ref[...] = _dot(xn, wb_ref[...])
    cgh_ref[...] = _dot(xn, wc_ref[...]) * _dot(xn, wh_ref[...])


def _inproj_cd(x, g, w, *, tm, tw):
    n, d = x.shape
    cdim = w.shape[1] // 5
    nct = cdim // tw
    out = jax.ShapeDtypeStruct((n, cdim), F32)
    ospec = pl.BlockSpec((tm, tw), lambda i, j: (i, j))
    return pl.pallas_call(
        _inproj_cd_kernel,
        grid=(n // tm, nct),
        in_specs=[pl.BlockSpec((tm, d), lambda i, j: (i, 0)),
                  _resident((1, d), lambda i, j: (0, 0))]
                 + [pl.BlockSpec((d, tw), lambda i, j, k=k: (0, k * nct + j)) for k in range(5)],
        out_specs=[ospec, ospec, ospec],
        out_shape=[out, out, out],
        scratch_shapes=[pltpu.VMEM((tm, d), BF16)],
        compiler_params=_params(("arbitrary", "arbitrary"), 48),
        name="inproj_cd",
    )(x, g, w, w, w, w, w)


def _ln_silu(y, g, b):
    mu = jnp.mean(y, axis=-1, keepdims=True)
    yc = y - mu
    var = jnp.mean(yc * yc, axis=-1, keepdims=True)
    z = yc * lax.rsqrt(var + EPS) * g + b
    return z * jax.nn.sigmoid(z)


def _conv_prompt_kernel(glu_ref, cgh_ref, bg_ref, cw_ref, cb_ref, lg_ref, lb_ref, sw_ref, o_ref,
                        ext_ref, sh_ref, ext2_ref, y_ref, *, tt, rc, lt):
    tb = pl.program_id(1)
    kw = cw_ref.shape[0]
    sk = sw_ref.shape[0]
    cdim = glu_ref.shape[1]
    sub = 8
    halo, halo2 = 32, 8

    @pl.when(tb == 0)
    def _():
        ext_ref[0:halo, :] = jnp.zeros((halo, cdim), F32)
        ext2_ref[0:halo2, :] = jnp.zeros((halo2, cdim), F32)

    ext_ref[halo:halo + tt, :] = glu_ref[...]
    ext2_ref[halo2:halo2 + tt, :] = cgh_ref[...]
    span = halo + tt - sub
    for j in range(1, sub):
        sh_ref[j - 1, 0:span, :] = ext_ref[j:j + span, :]

    for l0 in range(0, cdim, lt):
        ls = slice(l0, l0 + lt)
        for r0 in range(0, tt, rc):
            acc = jnp.zeros((rc, lt), F32)
            for k in range(kw):
                off = halo - (kw - 1) + k
                j, base = off % sub, r0 + off - off % sub
                rows = ext_ref[base:base + rc, ls] if j == 0 else sh_ref[j - 1, base:base + rc, ls]
                acc = acc + rows * cw_ref[k:k + 1, ls]
            y_ref[r0:r0 + rc, ls] = acc + cb_ref[:, ls]
            sc = jnp.zeros((rc, lt), F32)
            for k in range(sk):
                base = halo2 - (sk - 1) + r0 + k
                sc = sc + ext2_ref[base:base + rc, ls] * sw_ref[k:k + 1, ls]
            o_ref[r0:r0 + rc, cdim + l0:cdim + l0 + lt] = (bg_ref[r0:r0 + rc, ls] * sc).astype(o_ref.dtype)

    o_ref[:, 0:cdim] = _ln_silu(y_ref[...], lg_ref[...], lb_ref[...]).astype(o_ref.dtype)

    ext_ref[0:halo, :] = ext_ref[tt:tt + halo, :]
    ext2_ref[0:halo2, :] = ext2_ref[tt:tt + halo2, :]


def _conv_prompt(glu, cgh, bg, cw, cb, lg, lb, sw, *, seq, tt=256):
    n, cdim = glu.shape
    nt = seq // tt
    blk = lambda b, t: (b * nt + t, 0)
    c2 = lambda b, t: (0, 0)
    return pl.pallas_call(
        functools.partial(_conv_prompt_kernel, tt=tt, rc=tt // 2, lt=LANES),
        grid=(n // seq, nt),
        in_specs=[pl.BlockSpec((tt, cdim), blk), pl.BlockSpec((tt, cdim), blk), pl.BlockSpec((tt, cdim), blk),
                  _resident(cw.shape, c2), _resident((1, cdim), c2), _resident((1, cdim), c2),
                  _resident((1, cdim), c2), _resident(sw.shape, c2)],
        out_specs=pl.BlockSpec((tt, 2 * cdim), blk),
        out_shape=jax.ShapeDtypeStruct((n, 2 * cdim), BF16),
        scratch_shapes=[pltpu.VMEM((tt + 32, cdim), F32), pltpu.VMEM((7, tt + 24, cdim), F32),
                        pltpu.VMEM((tt + 8, cdim), F32), pltpu.VMEM((tt, cdim), F32)],
        compiler_params=_params(("arbitrary", "arbitrary"), 40),
        name="conv_prompt",
    )(glu, cgh, bg, cw, cb, lg, lb, sw)


def _conv_sample_kernel(cst_ref, sst_ref, glu_ref, cgh_ref, bg_ref, cw_ref, cb_ref, lg_ref, lb_ref, sw_ref, o_ref,
                        *, dec_seq):
    kw = cw_ref.shape[0]
    sk = sw_ref.shape[0]
    cdim = glu_ref.shape[2]

    def ext(r):
        return cst_ref[r] if r < kw - 1 else glu_ref[r - (kw - 1)]

    def ext2(r):
        return sst_ref[r] if r < sk - 1 else cgh_ref[r - (sk - 1)]

    for t in range(dec_seq):
        acc = ext(t) * cw_ref[0:1, :]
        for k in range(1, kw):
            acc = acc + ext(t + k) * cw_ref[k:k + 1, :]
        o_ref[t, :, 0:cdim] = _ln_silu(acc + cb_ref[...], lg_ref[...], lb_ref[...]).astype(o_ref.dtype)
        sc = ext2(t) * sw_ref[0:1, :]
        for k in range(1, sk):
            sc = sc + ext2(t + k) * sw_ref[k:k + 1, :]
        o_ref[t, :, cdim:2 * cdim] = (bg_ref[t] * sc).astype(o_ref.dtype)


def _conv_sample(cst, sst, glu, cgh, bg, cw, cb, lg, lb, sw, *, tb=32):
    t, b, cdim = glu.shape
    blk = lambda i: (0, i, 0)
    c2 = lambda i: (0, 0)
    new = pl.BlockSpec((t, tb, cdim), blk)
    return pl.pallas_call(
        functools.partial(_conv_sample_kernel, dec_seq=t),
        grid=(b // tb,),
        in_specs=[pl.BlockSpec((cst.shape[0], tb, cdim), blk), pl.BlockSpec((sst.shape[0], tb, cdim), blk),
                  new, new, new,
                  _resident(cw.shape, c2), _resident((1, cdim), c2), _resident((1, cdim), c2),
                  _resident((1, cdim), c2), _resident(sw.shape, c2)],
        out_specs=pl.BlockSpec((t, tb, 2 * cdim), blk),
        out_shape=jax.ShapeDtypeStruct((t, b, 2 * cdim), BF16),
        compiler_params=_params(("arbitrary",), 32),
        name="conv_sample",
    )(cst, sst, glu, cgh, bg, cw, cb, lg, lb, sw)


def _rope_tables(pos):
    half = QK_ROPE // 2
    inv = ROPE_THETA ** (-jnp.arange(half, dtype=F32) / half)
    ang = pos.astype(F32)[:, None] * inv[None, :]
    z = jnp.zeros((pos.shape[0], LANES - QK_ROPE), F32)
    cos, sin = jnp.cos(ang), jnp.sin(ang)
    return jnp.concatenate([cos, cos, z], axis=-1), jnp.concatenate([sin, sin, z], axis=-1)


def _rot_cols(w):
    half = QK_ROPE // 2
    return jnp.concatenate([-w[..., half:], w[..., :half]], axis=-1)


def _pad_cols(w, width):
    return jnp.concatenate([w, jnp.zeros(w.shape[:-1] + (width - w.shape[-1],), w.dtype)], axis=-1)


def _tail(buf, new, keep):
    new_tail = new[:, max(0, new.shape[1] - keep):]
    return jnp.concatenate([buf.astype(new.dtype), new_tail], axis=1)[:, -keep:]


def kernel(x_prompt, x_sample, cache_mla_latent, cache_mla_krope, page_table, state_pool, state_ccv, state_sconv,
           norm1_g, norm2_g, final_norm_g, w_in_ab, pool_w, pool_scale, q_norm_g, w_uq, kv_norm_g, w_uk, w_uv,
           w_out_ab, w_in_cd, ccv_w, ccv_b, ccv_ln_g, ccv_ln_b, sconv_w, w_out_cd, w_up, w_down):
    bsz, seq, d = x_prompt.shape
    dbsz, dseq, _ = x_sample.shape
    depth = norm1_g.shape[0]
    page = cache_mla_latent.shape[2]
    past_len = page_table.shape[1] * page
    heads = w_uq.shape[2]
    pd = state_pool.shape[-1]
    ql = q_norm_g.shape[-1]
    kvl = kv_norm_g.shape[-1]
    cdim = state_ccv.shape[-1]
    n_p, n_s = bsz * seq, dbsz * dseq
    tm_p = 256 if n_p % 256 == 0 else n_p
    tm_s = 256 if n_s % 256 == 0 else n_s
    tmm_p = 512 if n_p % 512 == 0 else tm_p
    tmm_s = 512 if n_s % 512 == 0 else tm_s

    hp = x_prompt.reshape(n_p, d)
    hs = x_sample.transpose(1, 0, 2).reshape(n_s, d)

    cos_p, sin_p = _rope_tables(jnp.arange(seq, dtype=jnp.int32))
    pos_s = jnp.repeat(past_len + jnp.arange(dseq, dtype=jnp.int32), dbsz)
    cos_s, sin_s = _rope_tables(pos_s)

    def row2(v):
        return v.reshape(1, -1)

    def to_bt(a):
        return a.reshape(dseq, dbsz, -1).transpose(1, 0, 2)

    wup = w_up.astype(BF16)
    wdn = w_down.astype(BF16)
    fg = row2(final_norm_g)
    p_lat, p_kr, s_lat, s_kr, p_pool, s_pool = [], [], [], [], [], []
    p_ccv, s_ccv, p_sc, s_sc = [], [], [], []
    for layer in range(depth):
        i = layer // 2
        last = layer == depth - 1
        if layer % 2 == 0:
            wi = w_in_ab[i]
            o3 = pd + ql + kvl
            k_r = wi[:, o3:]
            win = jnp.concatenate([wi[:, :o3], _pad_cols(k_r, LANES), _pad_cols(_rot_cols(k_r), LANES)], axis=-1).astype(BF16)
            wq_h = w_uq[i]
            wq = _pad_cols(wq_h, HEAD_PAD).reshape(ql, heads * HEAD_PAD).astype(BF16)
            wqr = _pad_cols(_rot_cols(wq_h[..., QK_NOPE:]), LANES).reshape(ql, heads * LANES).astype(BF16)
            wuk2 = w_uk[i].reshape(kvl, heads * QK_NOPE).astype(BF16)
            wuv2 = w_uv[i].reshape(kvl, heads * V_HEAD).astype(BF16)
            wuk_t = w_uk[i].transpose(1, 2, 0).astype(BF16)
            wuv_h = w_uv[i].transpose(1, 0, 2).astype(BF16)
            pw = pool_w[i].astype(BF16)
            ps = row2(pool_scale[i])
            wo = w_out_ab[i].astype(BF16)
            common = (row2(norm1_g[layer]), win, row2(q_norm_g[i]), row2(kv_norm_g[i]), wq, wqr)

            u, lat, kr, q, k, v = _inproj_ab(hp, *common, cos_p, sin_p, (wuk2, wuv2), heads=heads, sample=False,
                                             tm=tm_p, table_period=seq)
            a_out = _pool_prompt(u, pw, ps, seq=seq)
            b_out = _attn_prompt(q, k, v, seq=seq, heads=heads)
            parts_p = [a_out, b_out]
            p_lat.append(lat.reshape(bsz, seq, kvl))
            p_kr.append(kr.reshape(bsz, seq, QK_ROPE))
            p_pool.append(_tail(jnp.zeros((bsz, POOL_BUF, pd), F32), u.reshape(bsz, seq, pd), POOL_BUF))

            u, lat, kr, qlat, qrope = _inproj_ab(hs, *common, cos_s, sin_s, (wuk_t,), heads=heads, sample=True,
                                                 tm=tm_s, table_period=n_s)
            a_out = _pool_sample(state_pool[i].transpose(1, 0, 2), u.reshape(dseq, dbsz, pd), pw, ps, pos0=past_len)
            ql_b = qlat.reshape(dseq, dbsz, heads, kvl).transpose(1, 0, 2, 3).reshape(dbsz, dseq * heads, kvl)
            qr_b = (qrope.reshape(dseq, dbsz, heads, LANES)[..., :QK_ROPE]
                    .transpose(1, 0, 2, 3).reshape(dbsz, dseq * heads, QK_ROPE))
            lat_b, kr_b = to_bt(lat), to_bt(kr)
            o_lat = _attn_sample(page_table, ql_b, qr_b, lat_b, kr_b, cache_mla_latent[i],
                                 jnp.swapaxes(cache_mla_krope[i], 1, 2), heads=heads)
            o_tm = o_lat.reshape(dbsz, dseq, heads * kvl).transpose(1, 0, 2).reshape(n_s, heads * kvl)
            b_out = _unabsorb(o_tm, wuv_h, heads=heads, tm=tm_s)
            parts_s = [a_out.reshape(n_s, pd), b_out]
            s_lat.append(lat_b)
            s_kr.append(kr_b)
            s_pool.append(_tail(state_pool[i], to_bt(u), POOL_BUF))
        else:
            tw = 512
            wcd = w_in_cd[i].astype(BF16)
            conv_w = (ccv_w[i], row2(ccv_b[i]), row2(ccv_ln_g[i]), row2(ccv_ln_b[i]), sconv_w[i])
            wo = w_out_cd[i].astype(BF16)
            kw, sk = ccv_w.shape[1], sconv_w.shape[1]

            glu, cgh, bg = _inproj_cd(hp, row2(norm1_g[layer]), wcd, tm=tmm_p, tw=tw)
            cd = _conv_prompt(glu, cgh, bg, *conv_w, seq=seq)
            parts_p = [cd]
            p_ccv.append(_tail(jnp.zeros((bsz, kw - 1, cdim), F32), glu.reshape(bsz, seq, cdim), kw - 1))
            p_sc.append(_tail(jnp.zeros((bsz, sk - 1, cdim), F32), cgh.reshape(bsz, seq, cdim), sk - 1))

            glu, cgh, bg = _inproj_cd(hs, row2(norm1_g[layer]), wcd, tm=tmm_s, tw=tw)
            cd = _conv_sample(state_ccv[i].transpose(1, 0, 2), state_sconv[i].transpose(1, 0, 2),
                              glu.reshape(dseq, dbsz, cdim), cgh.reshape(dseq, dbsz, cdim),
                              bg.reshape(dseq, dbsz, cdim), *conv_w)
            parts_s = [cd.reshape(n_s, 2 * cdim)]
            s_ccv.append(_tail(state_ccv[i], to_bt(glu), kw - 1))
            s_sc.append(_tail(state_sconv[i], to_bt(cgh), sk - 1))

        mlp_w = (wo, row2(norm2_g[layer]), wup, wdn, fg)
        hp = _mix_mlp(hp, parts_p, *mlp_w, layer=layer, final=last, tm=tmm_p, tf=1024)
        hs = _mix_mlp(hs, parts_s, *mlp_w, layer=layer, final=last, tm=tmm_s, tf=1024)

    y_prompt = hp.reshape(bsz, seq, d)
    y_sample = to_bt(hs)
    return (y_prompt, y_sample,
            jnp.stack(p_lat), jnp.stack(p_kr), jnp.stack(s_lat), jnp.stack(s_kr),
            jnp.stack(p_pool), jnp.stack(s_pool),
            jnp.stack(p_ccv), jnp.stack(s_ccv),
            jnp.stack(p_sc), jnp.stack(s_sc))
```
